```python
import math
import jax, jax.numpy as jnp
from jax import lax
import numpy as np

D_MODEL = 2048
BATCH = 4
SEQ = 8192
DEPTH = 1

CHUNK = 64
Q_BLOCK = 128

MLA_HEADS = 8
MLA_Q_LORA = 512
MLA_KV_LORA = 256
MLA_NOPE = 128
MLA_ROPE = 64
MLA_V = 128
ROPE_THETA = 10000.0

FOX_HEADS = 8
FOX_HEAD_DIM = 128

D_FF = 4 * D_MODEL

EPS = 1e-6

OFF_CQ = 0
OFF_CKV = OFF_CQ + MLA_Q_LORA
OFF_KR = OFF_CKV + MLA_KV_LORA
OFF_FQ = OFF_KR + MLA_ROPE
OFF_FK = OFF_FQ + FOX_HEADS * FOX_HEAD_DIM
OFF_FV = OFF_FK + FOX_HEADS * FOX_HEAD_DIM
OFF_FF = OFF_FV + FOX_HEADS * FOX_HEAD_DIM
OFF_G = OFF_FF + FOX_HEADS
D_IN = OFF_G + 2 * D_MODEL

kernel_name = "hybrid_mla_fox_gated_block"


def rmsnorm(x, g):
    xf = x.astype(jnp.float32)
    y = xf * lax.rsqrt(jnp.mean(xf * xf, axis=-1, keepdims=True) + EPS)
    return (y * g.astype(jnp.float32)).astype(x.dtype)


def rope_tables(seq_len):
    pos = jnp.arange(seq_len, dtype=jnp.float32)
    inv = 1.0 / (ROPE_THETA ** (jnp.arange(0, MLA_ROPE, 2, dtype=jnp.float32) / MLA_ROPE))
    ang = pos[:, None] * inv[None, :]
    return jnp.cos(ang), jnp.sin(ang)


def apply_rope(x, cos, sin):
    half = x.shape[-1] // 2
    x1, x2 = x[..., :half], x[..., half:]
    c = cos.astype(x.dtype)
    s = sin.astype(x.dtype)
    return jnp.concatenate([x1 * c - x2 * s, x1 * s + x2 * c], axis=-1)


def block_attention(q, k, v, scale, chunk_causal, cum=None):
    B, H, S, Dq = q.shape
    Dv = v.shape[-1]
    nb = S // Q_BLOCK
    qb = q.reshape(B, H, nb, Q_BLOCK, Dq).transpose(2, 0, 1, 3, 4)
    k_pos = jnp.arange(S)
    xs = (jnp.arange(nb), qb)
    if cum is not None:
        cb = cum.reshape(B, H, nb, Q_BLOCK).transpose(2, 0, 1, 3)
        xs = xs + (cb,)

    def one_block(args):
        i, q_blk = args[0], args[1]
        s = jnp.einsum('bhqd,bhkd->bhqk', q_blk, k,
                       preferred_element_type=jnp.float32) * scale
        q_pos = i * Q_BLOCK + jnp.arange(Q_BLOCK)
        if chunk_causal:
            mask = (k_pos // CHUNK)[None, :] <= (q_pos // CHUNK)[:, None]
        else:
            mask = k_pos[None, :] <= q_pos[:, None]
        if cum is not None:
            c_blk = args[2]
            s = s + c_blk[..., :, None] - cum[:, :, None, :]
        s = jnp.where(mask, s, -jnp.inf)
        p = jax.nn.softmax(s, axis=-1)
        return jnp.einsum('bhqk,bhkd->bhqd', p.astype(v.dtype), v)

    out = lax.map(one_block, xs)
    return out.transpose(1, 2, 0, 3, 4).reshape(B, H, S, Dv)


def mla_branch(proj, q_norm, w_uq, kv_norm, w_ukv, cos, sin):
    B, S, _ = proj.shape
    cq = rmsnorm(proj[..., OFF_CQ:OFF_CKV], q_norm)
    q = (cq @ w_uq).reshape(B, S, MLA_HEADS, MLA_NOPE + MLA_ROPE).transpose(0, 2, 1, 3)
    ckv = rmsnorm(proj[..., OFF_CKV:OFF_KR], kv_norm)
    kv = (ckv @ w_ukv).reshape(B, S, MLA_HEADS, MLA_NOPE + MLA_V).transpose(0, 2, 1, 3)
    k_nope, v = kv[..., :MLA_NOPE], kv[..., MLA_NOPE:]
    k_rope = apply_rope(proj[:, None, :, OFF_KR:OFF_FQ], cos, sin)
    q = jnp.concatenate([q[..., :MLA_NOPE], apply_rope(q[..., MLA_NOPE:], cos, sin)], axis=-1)
    k = jnp.concatenate(
        [k_nope, jnp.broadcast_to(k_rope, (B, MLA_HEADS, S, MLA_ROPE))], axis=-1)
    o = block_attention(q, k, v, 1.0 / math.sqrt(MLA_NOPE + MLA_ROPE), chunk_causal=True)
    return o.transpose(0, 2, 1, 3).reshape(B, S, MLA_HEADS * MLA_V)


def fox_branch(proj, f_bias):
    B, S, _ = proj.shape

    def heads(a):
        return a.reshape(B, S, FOX_HEADS, FOX_HEAD_DIM).transpose(0, 2, 1, 3)

    q = heads(proj[..., OFF_FQ:OFF_FK])
    k = heads(proj[..., OFF_FK:OFF_FV])
    v = heads(proj[..., OFF_FV:OFF_FF])
    logf = jax.nn.log_sigmoid((proj[..., OFF_FF:OFF_G] + f_bias).astype(jnp.float32))
    cum = jnp.cumsum(logf, axis=1).transpose(0, 2, 1)
    o = block_attention(q, k, v, 1.0 / math.sqrt(FOX_HEAD_DIM), chunk_causal=False, cum=cum)
    return o.transpose(0, 2, 1, 3).reshape(B, S, FOX_HEADS * FOX_HEAD_DIM)


def setup_inputs(seed: int = 0) -> dict:
    key = jax.random.key(seed)
    ks = jax.random.split(key, 20)
    f32 = jnp.float32

    def w(k, shape, fan_in):
        return jax.random.normal(k, shape, f32) * (fan_in ** -0.5)

    def gain(k, shape):
        return 1.0 + 0.02 * jax.random.normal(k, shape, f32)

    return {
        "x": jax.random.normal(ks[0], (BATCH, SEQ, D_MODEL), f32),
        "attn_norm": gain(ks[1], (DEPTH, D_MODEL)),
        "w_in": w(ks[2], (DEPTH, D_MODEL, D_IN), D_MODEL),
        "fox_f_bias": jax.random.uniform(ks[3], (DEPTH, FOX_HEADS), f32, 1.0, 6.0),
        "q_norm": gain(ks[4], (DEPTH, MLA_Q_LORA)),
        "w_uq": w(ks[5], (DEPTH, MLA_Q_LORA, MLA_HEADS * (MLA_NOPE + MLA_ROPE)), MLA_Q_LORA),
        "kv_norm": gain(ks[6], (DEPTH, MLA_KV_LORA)),
        "w_ukv": w(ks[7], (DEPTH, MLA_KV_LORA, MLA_HEADS * (MLA_NOPE + MLA_V)), MLA_KV_LORA),
        "w_mla_branch": w(ks[8], (DEPTH, MLA_HEADS * MLA_V, D_MODEL), MLA_HEADS * MLA_V),
        "w_fox_branch": w(ks[9], (DEPTH, FOX_HEADS * FOX_HEAD_DIM, D_MODEL), FOX_HEADS * FOX_HEAD_DIM),
        "w_out": w(ks[10], (DEPTH, D_MODEL, D_MODEL), D_MODEL),
        "mlp_norm": gain(ks[11], (DEPTH, D_MODEL)),
        "w_up": w(ks[12], (DEPTH, D_MODEL, D_FF), D_MODEL),
        "w_down": w(ks[13], (DEPTH, D_FF, D_MODEL), D_FF),
        "final_norm": gain(ks[14], (D_MODEL,)),
    }


def reference(x, attn_norm, w_in, fox_f_bias, q_norm, w_uq, kv_norm, w_ukv,
              w_mla_branch, w_fox_branch, w_out, mlp_norm, w_up, w_down, final_norm):
    S = x.shape[1]
    cos, sin = rope_tables(S)
    h = x
    for l in range(DEPTH):
        xn = rmsnorm(h, attn_norm[l])
        proj = xn @ w_in[l]
        y_mla = mla_branch(proj, q_norm[l], w_uq[l], kv_norm[l], w_ukv[l], cos, sin) @ w_mla_branch[l]
        y_fox = fox_branch(proj, fox_f_bias[l]) @ w_fox_branch[l]
        gates = jax.nn.sigmoid(proj[..., OFF_G:])
        g_mla, g_fox = gates[..., :D_MODEL], gates[..., D_MODEL:]
        h = h + (g_mla * y_mla + g_fox * y_fox) @ w_out[l]
        hn = rmsnorm(h, mlp_norm[l])
        u = jnp.square(jax.nn.relu(hn @ w_up[l]))
        h = h + u @ w_down[l]
    return rmsnorm(h, final_norm)
```

```python
import functools
import math

import jax
import jax.numpy as jnp
from jax import lax
from jax.experimental import pallas as pl
from jax.experimental.pallas import tpu as pltpu

F32 = jnp.float32
BF16 = jnp.bfloat16

EPS = 1e-6
ROPE_THETA = 10000.0
N_HEADS = 8
HEAD = 128
ROPE = 64
Q_LORA = 512
KV_LORA = 256
CHUNK = 64

LANES = 128
V7X_VMEM_LIMIT = 56 * 1024 * 1024

TILE_PREP = 512
TILE_ATTN = 512
TILE_PROJ_M = 1024
TILE_PROJ_N = 1024
TILE_MIX = 512
TILE_MLP_M = 1024
TILE_MLP_F = 512

_NT = (((1,), (1,)), ((), ()))


def _rms(v, g):
    return v * lax.rsqrt(jnp.mean(v * v, axis=-1, keepdims=True) + EPS) * g


def _params(sem, vmem=V7X_VMEM_LIMIT):
    return pltpu.CompilerParams(dimension_semantics=sem, vmem_limit_bytes=vmem)


def _resident(shape):
    nd = len(shape)
    return pl.BlockSpec(shape, lambda *_: (0,) * nd, pipeline_mode=pl.Buffered(1))


def _prep_kernel(x_ref, g_ref, wa_ref, qg_ref, wq_ref, kg_ref, wk_ref, wvt_ref,
                 wfvt_ref, fb_ref, cos_ref, sin_ref,
                 xn_ref, q1_ref, q2_ref, k1_ref, k2_ref, vt_ref, fvt_ref,
                 fq2_ref, fk2_ref, carry_ref, *, scale):
    tm = x_ref.shape[0]
    xn = _rms(x_ref[...], g_ref[...]).astype(BF16)
    xn_ref[...] = xn
    a = jnp.dot(xn, wa_ref[...], preferred_element_type=F32)
    cos = cos_ref[...]
    sin = sin_ref[...]

    def rope(r):
        return r * cos + pltpu.roll(r, ROPE, 1) * sin

    cqn = _rms(a[:, :Q_LORA], qg_ref[...]).astype(BF16)
    q = jnp.dot(cqn, wq_ref[...], preferred_element_type=F32)
    nq = N_HEADS * HEAD
    q1_ref[...] = (q[:, :nq] * scale).astype(BF16)
    for h in range(N_HEADS):
        r = q[:, nq + h * LANES: nq + (h + 1) * LANES]
        q2_ref[:, h * LANES:(h + 1) * LANES] = (rope(r) * scale).astype(BF16)

    ckvn = _rms(a[:, Q_LORA:Q_LORA + KV_LORA], kg_ref[...]).astype(BF16)
    k1_ref[...] = jnp.dot(ckvn, wk_ref[...], preferred_element_type=F32).astype(BF16)
    vt_ref[...] = lax.dot_general(wvt_ref[...], ckvn, _NT,
                                  preferred_element_type=F32).astype(BF16)
    off = Q_LORA + KV_LORA
    k2_ref[...] = rope(a[:, off:off + LANES]).astype(BF16)
    fvt_ref[...] = lax.dot_general(wfvt_ref[...], xn, _NT,
                                   preferred_element_type=F32).astype(BF16)

    z = a[:, off + LANES:off + 2 * LANES] + fb_ref[...]
    logf = jnp.minimum(z, 0.0) - jnp.log1p(jnp.exp(-jnp.abs(z)))

    @pl.when(pl.program_id(1) == 0)
    def _():
        carry_ref[...] = jnp.zeros_like(carry_ref)

    row = lax.broadcasted_iota(jnp.int32, (tm, tm), 0)
    col = lax.broadcasted_iota(jnp.int32, (tm, tm), 1)
    tri = (row >= col).astype(F32)
    cum = jnp.dot(tri, logf, preferred_element_type=F32,
                  precision=lax.Precision.HIGHEST) + carry_ref[...]
    carry_ref[...] = cum[tm - 1:tm, :]

    lane = lax.broadcasted_iota(jnp.int32, (tm, LANES), 1)
    for h in range(N_HEADS):
        c = cum[:, h:h + 1]
        hi = c.astype(BF16).astype(F32)
        r1 = c - hi
        mid = r1.astype(BF16).astype(F32)
        lo = r1 - mid
        qa = jnp.where(lane == 0, hi, jnp.where(lane == 1, mid, jnp.where(
            lane == 2, lo, jnp.where(lane < 6, 1.0, 0.0))))
        ka = jnp.where(lane < 3, 1.0, jnp.where(lane == 3, -hi, jnp.where(
            lane == 4, -mid, jnp.where(lane == 5, -lo, 0.0))))
        fq2_ref[:, h * LANES:(h + 1) * LANES] = qa.astype(BF16)
        fk2_ref[:, h * LANES:(h + 1) * LANES] = ka.astype(BF16)


def _prep(x, attn_g, w_a, q_g, w_q, kv_g, w_k, w_vt, w_fvt, f_bias, cos, sin, scale):
    B, S, D = x.shape
    tm = min(TILE_PREP, S)
    nb = S // tm
    hd = N_HEADS * HEAD
    row = lambda w: pl.BlockSpec((None, tm, w), lambda b, i: (b, i, 0))
    vt_spec = pl.BlockSpec((None, None, hd, tm), lambda b, i: (b, i, 0, 0))
    tab = pl.BlockSpec((tm, LANES), lambda b, i: (i, 0))
    out_shape = [
        jax.ShapeDtypeStruct((B, S, D), BF16),
        jax.ShapeDtypeStruct((B, S, hd), BF16),
        jax.ShapeDtypeStruct((B, S, hd), BF16),
        jax.ShapeDtypeStruct((B, S, hd), BF16),
        jax.ShapeDtypeStruct((B, S, LANES), BF16),
        jax.ShapeDtypeStruct((B, nb, hd, tm), BF16),
        jax.ShapeDtypeStruct((B, nb, hd, tm), BF16),
        jax.ShapeDtypeStruct((B, S, hd), BF16),
        jax.ShapeDtypeStruct((B, S, hd), BF16),
    ]
    return pl.pallas_call(
        functools.partial(_prep_kernel, scale=scale),
        grid=(B, nb),
        in_specs=[row(D), _resident(attn_g.shape), _resident(w_a.shape),
                  _resident(q_g.shape), _resident(w_q.shape), _resident(kv_g.shape),
                  _resident(w_k.shape), _resident(w_vt.shape), _resident(w_fvt.shape),
                  _resident(f_bias.shape), tab, tab],
        out_specs=[row(D), row(hd), row(hd), row(hd), row(LANES), vt_spec, vt_spec,
                   row(hd), row(hd)],
        out_shape=out_shape,
        scratch_shapes=[pltpu.VMEM((1, LANES), F32)],
        compiler_params=_params(("arbitrary", "arbitrary")),
    )(x, attn_g, w_a, q_g, w_q, kv_g, w_k, w_vt, w_fvt, f_bias, cos, sin)


def _proj_kernel(xn_ref, w_ref, o_ref, *, n_gate, n_q, scale):
    j = pl.program_id(2)
    acc = jnp.dot(xn_ref[...], w_ref[...], preferred_element_type=F32)

    @pl.when(j < n_gate)
    def _():
        o_ref[...] = jax.nn.sigmoid(acc).astype(BF16)

    @pl.when(j >= n_gate)
    def _():
        fac = jnp.where(j < n_gate + n_q, scale, 1.0).astype(F32)
        o_ref[...] = (acc * fac).astype(BF16)


def _proj(xn, w_b, n_gate_cols, n_q_cols, scale):
    B, S, D = xn.shape
    N = w_b.shape[1]
    tm = min(TILE_PROJ_M, S)
    tn = TILE_PROJ_N
    return pl.pallas_call(
        functools.partial(_proj_kernel, n_gate=n_gate_cols // tn, n_q=n_q_cols // tn,
                          scale=scale),
        grid=(B, S // tm, N // tn),
        in_specs=[pl.BlockSpec((None, tm, D), lambda b, i, j: (b, i, 0)),
                  pl.BlockSpec((D, tn), lambda b, i, j: (0, j))],
        out_specs=pl.BlockSpec((None, tm, tn), lambda b, i, j: (b, i, j)),
        out_shape=jax.ShapeDtypeStruct((B, S, N), BF16),
        compiler_params=_params(("arbitrary", "arbitrary", "arbitrary")),
    )(xn, w_b)


def _attn_kernel(q1_ref, q2_ref, k1_ref, k2_ref, vt_ref, o_ref, m_ref, l_ref, acc_ref,
                 *, chunk_causal):
    t = q1_ref.shape[0]
    i = pl.program_id(2)
    q = jnp.concatenate([q1_ref[...], q2_ref[...]], axis=1)
    m_ref[...] = jnp.full_like(m_ref, -jnp.inf)
    l_ref[...] = jnp.zeros_like(l_ref)
    acc_ref[...] = jnp.zeros_like(acc_ref)

    def step(kb, masked):
        start = pl.multiple_of(kb * t, t)
        k = jnp.concatenate([k1_ref[pl.ds(start, t), :], k2_ref[pl.ds(start, t), :]],
                            axis=1)
        s = lax.dot_general(k, q, _NT, preferred_element_type=F32)
        if masked:
            kj = lax.broadcasted_iota(jnp.int32, (t, t), 0)
            qi = lax.broadcasted_iota(jnp.int32, (t, t), 1)
            if chunk_causal:
                shift = CHUNK.bit_length() - 1
                valid = (kj >> shift) <= (qi >> shift)
            else:
                valid = kj <= qi
            s = jnp.where(valid, s, -jnp.inf)
        m_prev = m_ref[...]
        m_new = jnp.maximum(m_prev, jnp.max(s, axis=0, keepdims=True))
        alpha = jnp.exp(m_prev - m_new)
        p = jnp.exp(s - m_new)
        l_ref[...] = alpha * l_ref[...] + jnp.sum(p, axis=0, keepdims=True)
        pv = jnp.dot(vt_ref[kb], p.astype(BF16), preferred_element_type=F32)
        acc_ref[...] = alpha * acc_ref[...] + pv
        m_ref[...] = m_new

    def body(kb, carry):
        step(kb, False)
        return carry

    lax.fori_loop(0, i, body, 0)
    step(i, True)
    out = acc_ref[...] * (1.0 / l_ref[...])
    o_ref[...] = out.T.astype(BF16)


def _attention(q1, q1_off, q2, k1, k1_off, k2, k2_shared, vt, chunk_causal):
    B, S = q1.shape[0], q1.shape[1]
    t = min(TILE_ATTN, S)
    nq = S // t
    assert vt.shape[1] == nq and vt.shape[3] == t
    qspec = lambda off: pl.BlockSpec((None, t, LANES), lambda b, h, i: (b, i, h + off))
    kspec = lambda off: pl.BlockSpec((None, S, LANES), lambda b, h, i: (b, 0, h + off))
    k2spec = (pl.BlockSpec((None, S, LANES), lambda b, h, i: (b, 0, 0))
              if k2_shared else kspec(0))
    return pl.pallas_call(
        functools.partial(_attn_kernel, chunk_causal=chunk_causal),
        grid=(B, N_HEADS, nq),
        in_specs=[qspec(q1_off), qspec(0), kspec(k1_off), k2spec,
                  pl.BlockSpec((None, nq, HEAD, t), lambda b, h, i: (b, 0, h, 0))],
        out_specs=pl.BlockSpec((None, t, HEAD), lambda b, h, i: (b, i, h)),
        out_shape=jax.ShapeDtypeStruct((B, S, N_HEADS * HEAD), BF16),
        scratch_shapes=[pltpu.VMEM((1, t), F32), pltpu.VMEM((1, t), F32),
                        pltpu.VMEM((HEAD, t), F32)],
        compiler_params=_params(("arbitrary", "arbitrary", "arbitrary")),
    )(q1, q2, k1, k2, vt)


def _mix_kernel(om_ref, of_ref, gm_ref, gf_ref, x_ref, wm_ref, wf_ref, wo_ref, h_ref):
    y_m = jnp.dot(om_ref[...], wm_ref[...], preferred_element_type=F32)
    y_f = jnp.dot(of_ref[...], wf_ref[...], preferred_element_type=F32)
    z = (gm_ref[...].astype(F32) * y_m + gf_ref[...].astype(F32) * y_f).astype(BF16)
    h_ref[...] = x_ref[...] + jnp.dot(z, wo_ref[...], preferred_element_type=F32)


def _mix(o_mla, o_fox, proj, x, w_m, w_f, w_o):
    B, S, D = x.shape
    tm = min(TILE_MIX, S)
    hd = o_mla.shape[2]
    row = lambda w, c=0: pl.BlockSpec((None, tm, w), lambda b, i: (b, i, c))
    return pl.pallas_call(
        _mix_kernel,
        grid=(B, S // tm),
        in_specs=[row(hd), row(hd), row(D, 0), row(D, 1), row(D),
                  _resident(w_m.shape), _resident(w_f.shape), _resident(w_o.shape)],
        out_specs=row(D),
        out_shape=jax.ShapeDtypeStruct((B, S, D), F32),
        compiler_params=_params(("arbitrary", "arbitrary")),
    )(o_mla, o_fox, proj, proj, x, w_m, w_f, w_o)


def _mlp_kernel(h_ref, g_ref, wu_ref, wd_ref, fg_ref, o_ref, hn_ref):
    j = pl.program_id(2)

    @pl.when(j == 0)
    def _():
        h = h_ref[...]
        hn_ref[...] = _rms(h, g_ref[...]).astype(BF16)
        o_ref[...] = h

    u = jnp.dot(hn_ref[...], wu_ref[...], preferred_element_type=F32)
    u = jnp.square(jnp.maximum(u, 0.0)).astype(BF16)
    o_ref[...] += jnp.dot(u, wd_ref[...], preferred_element_type=F32)

    @pl.when(j == pl.num_programs(2) - 1)
    def _():
        o_ref[...] = _rms(o_ref[...], fg_ref[...])


def _mlp(h, mlp_g, w_up, w_down, final_g):
    B, S, D = h.shape
    Fdim = w_up.shape[1]
    tm = min(TILE_MLP_M, S)
    tf = TILE_MLP_F
    row = pl.BlockSpec((None, tm, D), lambda b, i, j: (b, i, 0))
    return pl.pallas_call(
        _mlp_kernel,
        grid=(B, S // tm, Fdim // tf),
        in_specs=[row, _resident(mlp_g.shape),
                  pl.BlockSpec((D, tf), lambda b, i, j: (0, j)),
                  pl.BlockSpec((tf, D), lambda b, i, j: (j, 0)),
                  _resident(final_g.shape)],
        out_specs=row,
        out_shape=jax.ShapeDtypeStruct((B, S, D), F32),
        scratch_shapes=[pltpu.VMEM((tm, D), BF16)],
        compiler_params=_params(("arbitrary", "arbitrary", "arbitrary")),
    )(h, mlp_g, w_up, w_down, final_g)


def _layer_weights(w_in, w_uq, w_ukv, fox_f_bias):
    D = w_in.shape[0]
    hd = N_HEADS * HEAD
    o_ckv = Q_LORA
    o_kr = o_ckv + KV_LORA
    o_fq = o_kr + ROPE
    o_fk = o_fq + hd
    o_fv = o_fk + hd
    o_ff = o_fv + hd
    o_g = o_ff + N_HEADS
    half = ROPE // 2
    kr = w_in[:, o_kr:o_fq]
    kr4 = jnp.concatenate([kr, -kr[:, half:], kr[:, :half]], axis=1)
    ff = jnp.pad(w_in[:, o_ff:o_g], ((0, 0), (0, LANES - N_HEADS)))
    w_a = jnp.concatenate([w_in[:, :o_kr], kr4, ff], axis=1).astype(BF16)
    w_b = jnp.concatenate([w_in[:, o_g:], w_in[:, o_fq:o_fv]], axis=1).astype(BF16)
    w_fvt = w_in[:, o_fv:o_ff].T.astype(BF16)

    uq = w_uq.reshape(Q_LORA, N_HEADS, HEAD + ROPE)
    x1 = uq[:, :, HEAD:HEAD + half]
    x2 = uq[:, :, HEAD + half:]
    rope4 = jnp.concatenate([x1, x2, -x2, x1], axis=2).reshape(Q_LORA, hd)
    w_q = jnp.concatenate([uq[:, :, :HEAD].reshape(Q_LORA, hd), rope4], axis=1).astype(BF16)

    ukv = w_ukv.reshape(KV_LORA, N_HEADS, 2 * HEAD)
    w_k = ukv[:, :, :HEAD].reshape(KV_LORA, hd).astype(BF16)
    w_vt = ukv[:, :, HEAD:].reshape(KV_LORA, hd).T.astype(BF16)
    f_bias = jnp.pad(fox_f_bias, (0, LANES - N_HEADS)).reshape(1, LANES)
    return w_a, w_b, w_fvt, w_q, w_k, w_vt, f_bias


def _rope_tables(S):
    pos = jnp.arange(S, dtype=F32)
    inv = 1.0 / (ROPE_THETA ** (jnp.arange(0, ROPE, 2, dtype=F32) / ROPE))
    ang = pos[:, None] * inv[None, :]
    zeros = jnp.zeros((S, LANES - ROPE), F32)
    cos = jnp.concatenate([jnp.cos(ang), jnp.cos(ang), zeros], axis=1)
    sin = jnp.concatenate([jnp.sin(ang), jnp.sin(ang), zeros], axis=1)
    return cos, sin


def kernel(x, attn_norm, w_in, fox_f_bias, q_norm, w_uq, kv_norm, w_ukv, w_mla_branch,
           w_fox_branch, w_out, mlp_norm, w_up, w_down, final_norm):
    B, S, D = x.shape
    depth = w_in.shape[0]
    hd = N_HEADS * HEAD
    cos, sin = _rope_tables(S)
    mla_scale = 1.0 / math.sqrt(HEAD + ROPE)
    fox_scale = 1.0 / math.sqrt(HEAD)
    h = x
    for l in range(depth):
        w_a, w_b, w_fvt, w_q, w_k, w_vt, f_bias = _layer_weights(
            w_in[l], w_uq[l], w_ukv[l], fox_f_bias[l])
        (xn, q1, q2, k1, k2, vt, fvt, fq2, fk2) = _prep(
            h, attn_norm[l].reshape(1, D), w_a, q_norm[l].reshape(1, Q_LORA), w_q,
            kv_norm[l].reshape(1, KV_LORA), w_k, w_vt, w_fvt, f_bias, cos, sin, mla_scale)
        proj = _proj(xn, w_b, 2 * D, hd, fox_scale)
        o_mla = _attention(q1, 0, q2, k1, 0, k2, True, vt, chunk_causal=True)
        fq_blk = 2 * D // LANES
        o_fox = _attention(proj, fq_blk, fq2, proj, fq_blk + N_HEADS, fk2, False, fvt,
                           chunk_causal=False)
        h = _mix(o_mla, o_fox, proj, h, w_mla_branch[l].astype(BF16),
                 w_fox_branch[l].astype(BF16), w_out[l].astype(BF16))
        fg = final_norm if l == depth - 1 else None
        assert fg is not None, "DEPTH > 1 needs an un-normalised MLP epilogue"
        h = _mlp(h, mlp_norm[l].reshape(1, D), w_up[l].astype(BF16),
                 w_down[l].astype(BF16), fg.reshape(1, D))
    return h
```

```python
import functools
import math

import jax
import jax.numpy as jnp
from jax import lax
from jax.experimental import pallas as pl
from jax.experimental.pallas import tpu as pltpu

F32 = jnp.float32
BF16 = jnp.bfloat16

EPS = 1e-6
LOG2E = math.log2(math.e)
ROPE_THETA = 10000.0
N_HEADS = 8
HEAD = 128
ROPE = 64
Q_LORA = 512
KV_LORA = 256
CHUNK = 64

LANES = 128
V7X_VMEM_LIMIT = 56 * 1024 * 1024

TILE_PREP = 512
TILE_ATTN = 512
ATTN_HEADS = 2
TILE_PROJ_M = 1024
TILE_PROJ_N = 1024
TILE_MIX = 512
TILE_MLP_M = 1024
TILE_MLP_F = 512

_NT = (((1,), (1,)), ((), ()))


def _rms(v, g):
    return v * lax.rsqrt(jnp.mean(v * v, axis=-1, keepdims=True) + EPS) * g


def _params(sem, vmem=V7X_VMEM_LIMIT):
    return pltpu.CompilerParams(dimension_semantics=sem, vmem_limit_bytes=vmem)


def _resident(shape):
    nd = len(shape)
    return pl.BlockSpec(shape, lambda *_: (0,) * nd, pipeline_mode=pl.Buffered(1))


def _prep_kernel(x_ref, g_ref, wa_ref, qg_ref, wq_ref, kg_ref, wk_ref, wvt_ref,
                 wfvt_ref, fb_ref, cos_ref, sin_ref,
                 xn_ref, q1_ref, q2_ref, k1_ref, k2_ref, vt_ref, fvt_ref,
                 fq2_ref, fk2_ref, carry_ref, *, scale):
    tm = x_ref.shape[0]
    xn = _rms(x_ref[...], g_ref[...]).astype(BF16)
    xn_ref[...] = xn
    a = jnp.dot(xn, wa_ref[...], preferred_element_type=F32)
    cos = cos_ref[...]
    sin = sin_ref[...]

    def rope(r):
        return r * cos + pltpu.roll(r, ROPE, 1) * sin

    cqn = _rms(a[:, :Q_LORA], qg_ref[...]).astype(BF16)
    q = jnp.dot(cqn, wq_ref[...], preferred_element_type=F32)
    nq = N_HEADS * HEAD
    q1_ref[...] = (q[:, :nq] * scale).astype(BF16)
    for h in range(N_HEADS):
        r = q[:, nq + h * LANES: nq + (h + 1) * LANES]
        q2_ref[:, h * LANES:(h + 1) * LANES] = (rope(r) * scale).astype(BF16)

    ckvn = _rms(a[:, Q_LORA:Q_LORA + KV_LORA], kg_ref[...]).astype(BF16)
    k1_ref[...] = jnp.dot(ckvn, wk_ref[...], preferred_element_type=F32).astype(BF16)
    vt_ref[...] = lax.dot_general(wvt_ref[...], ckvn, _NT,
                                  preferred_element_type=F32).astype(BF16)
    off = Q_LORA + KV_LORA
    k2_ref[...] = rope(a[:, off:off + LANES]).astype(BF16)
    fvt_ref[...] = lax.dot_general(wfvt_ref[...], xn, _NT,
                                   preferred_element_type=F32).astype(BF16)

    z = a[:, off + LANES:off + 2 * LANES] + fb_ref[...]
    logf = jnp.minimum(z, 0.0) - jnp.log1p(jnp.exp(-jnp.abs(z)))

    @pl.when(pl.program_id(1) == 0)
    def _():
        carry_ref[...] = jnp.zeros_like(carry_ref)

    row = lax.broadcasted_iota(jnp.int32, (tm, tm), 0)
    col = lax.broadcasted_iota(jnp.int32, (tm, tm), 1)
    tri = (row >= col).astype(F32)
    cum = jnp.dot(tri, logf, preferred_element_type=F32,
                  precision=lax.Precision.HIGHEST) + carry_ref[...]
    carry_ref[...] = cum[tm - 1:tm, :]

    lane = lax.broadcasted_iota(jnp.int32, (tm, LANES), 1)
    for h in range(N_HEADS):
        c = cum[:, h:h + 1] * LOG2E
        hi = c.astype(BF16).astype(F32)
        r1 = c - hi
        mid = r1.astype(BF16).astype(F32)
        lo = r1 - mid
        qa = jnp.where(lane == 0, hi, jnp.where(lane == 1, mid, jnp.where(
            lane == 2, lo, jnp.where(lane < 6, 1.0, 0.0))))
        ka = jnp.where(lane < 3, 1.0, jnp.where(lane == 3, -hi, jnp.where(
            lane == 4, -mid, jnp.where(lane == 5, -lo, 0.0))))
        fq2_ref[:, h * LANES:(h + 1) * LANES] = qa.astype(BF16)
        fk2_ref[:, h * LANES:(h + 1) * LANES] = ka.astype(BF16)


def _prep(x, attn_g, w_a, q_g, w_q, kv_g, w_k, w_vt, w_fvt, f_bias, cos, sin, scale):
    B, S, D = x.shape
    tm = min(TILE_PREP, S)
    nb = S // tm
    hd = N_HEADS * HEAD
    row = lambda w: pl.BlockSpec((None, tm, w), lambda b, i: (b, i, 0))
    vt_spec = pl.BlockSpec((None, None, hd, tm), lambda b, i: (b, i, 0, 0))
    tab = pl.BlockSpec((tm, LANES), lambda b, i: (i, 0))
    out_shape = [
        jax.ShapeDtypeStruct((B, S, D), BF16),
        jax.ShapeDtypeStruct((B, S, hd), BF16),
        jax.ShapeDtypeStruct((B, S, hd), BF16),
        jax.ShapeDtypeStruct((B, S, hd), BF16),
        jax.ShapeDtypeStruct((B, S, LANES), BF16),
        jax.ShapeDtypeStruct((B, nb, hd, tm), BF16),
        jax.ShapeDtypeStruct((B, nb, hd, tm), BF16),
        jax.ShapeDtypeStruct((B, S, hd), BF16),
        jax.ShapeDtypeStruct((B, S, hd), BF16),
    ]
    return pl.pallas_call(
        functools.partial(_prep_kernel, scale=scale),
        grid=(B, nb),
        in_specs=[row(D), _resident(attn_g.shape), _resident(w_a.shape),
                  _resident(q_g.shape), _resident(w_q.shape), _resident(kv_g.shape),
                  _resident(w_k.shape), _resident(w_vt.shape), _resident(w_fvt.shape),
                  _resident(f_bias.shape), tab, tab],
        out_specs=[row(D), row(hd), row(hd), row(hd), row(LANES), vt_spec, vt_spec,
                   row(hd), row(hd)],
        out_shape=out_shape,
        scratch_shapes=[pltpu.VMEM((1, LANES), F32)],
        compiler_params=_params(("arbitrary", "arbitrary")),
    )(x, attn_g, w_a, q_g, w_q, kv_g, w_k, w_vt, w_fvt, f_bias, cos, sin)


def _proj_kernel(xn_ref, w_ref, o_ref, *, n_gate, n_q, scale):
    j = pl.program_id(2)
    acc = jnp.dot(xn_ref[...], w_ref[...], preferred_element_type=F32)

    @pl.when(j < n_gate)
    def _():
        o_ref[...] = jax.nn.sigmoid(acc).astype(BF16)

    @pl.when(j >= n_gate)
    def _():
        fac = jnp.where(j < n_gate + n_q, scale, 1.0).astype(F32)
        o_ref[...] = (acc * fac).astype(BF16)


def _proj(xn, w_b, n_gate_cols, n_q_cols, scale):
    B, S, D = xn.shape
    N = w_b.shape[1]
    tm = min(TILE_PROJ_M, S)
    tn = TILE_PROJ_N
    return pl.pallas_call(
        functools.partial(_proj_kernel, n_gate=n_gate_cols // tn, n_q=n_q_cols // tn,
                          scale=scale),
        grid=(B, S // tm, N // tn),
        in_specs=[pl.BlockSpec((None, tm, D), lambda b, i, j: (b, i, 0)),
                  pl.BlockSpec((D, tn), lambda b, i, j: (0, j))],
        out_specs=pl.BlockSpec((None, tm, tn), lambda b, i, j: (b, i, j)),
        out_shape=jax.ShapeDtypeStruct((B, S, N), BF16),
        compiler_params=_params(("arbitrary", "arbitrary", "arbitrary")),
    )(xn, w_b)


def _attn_kernel(q1_ref, q2_ref, k1_ref, k2_ref, vt_ref, o_ref, *scratch,
                 chunk_causal, k2_shared):
    t = q1_ref.shape[0]
    i = pl.program_id(2)
    per_head = len(scratch) // ATTN_HEADS

    def lanes(ref, g, rows=slice(None)):
        return ref[rows, g * LANES:(g + 1) * LANES]

    class Head:
        def __init__(self, g):
            (self.q, s0, s1, b0, b1, p0, p1, a0, a1, self.m, self.l,
             self.acc) = scratch[g * per_head:(g + 1) * per_head]
            self.g = g
            self.s = (s0, s1)
            self.b = (b0, b1)
            self.p = (p0, p1)
            self.a = (a0, a1)

        def init(self):
            self.q[...] = jnp.concatenate([lanes(q1_ref, self.g), lanes(q2_ref, self.g)],
                                          axis=1)
            self.m[...] = jnp.full_like(self.m, -jnp.inf)
            self.l[...] = jnp.zeros_like(self.l)
            self.acc[...] = jnp.zeros_like(self.acc)
            self.p[1][...] = jnp.zeros_like(self.p[1])
            self.a[1][...] = jnp.ones_like(self.a[1])

        def scores(self, kb, slot):
            rows = pl.ds(pl.multiple_of(kb * t, t), t)
            k2 = k2_ref[rows, :] if k2_shared else lanes(k2_ref, self.g, rows)
            k = jnp.concatenate([lanes(k1_ref, self.g, rows), k2], axis=1)
            s = lax.dot_general(k, self.q[...], _NT, preferred_element_type=F32)
            self.s[slot][...] = s
            self.b[slot][...] = jnp.max(s, axis=0, keepdims=True)

        def softmax(self, slot, masked):
            s = self.s[slot][...]
            if masked:
                kj = lax.broadcasted_iota(jnp.int32, (t, t), 0)
                qi = lax.broadcasted_iota(jnp.int32, (t, t), 1)
                if chunk_causal:
                    shift = CHUNK.bit_length() - 1
                    valid = (kj >> shift) <= (qi >> shift)
                else:
                    valid = kj <= qi
                s = jnp.where(valid, s, -jnp.inf)
                blk_max = jnp.max(s, axis=0, keepdims=True)
            else:
                blk_max = self.b[slot][...]
            m_prev = self.m[...]
            m_new = jnp.maximum(m_prev, blk_max)
            alpha = jnp.exp2(m_prev - m_new)
            p = jnp.exp2(s - m_new)
            self.l[...] = alpha * self.l[...] + jnp.sum(p, axis=0, keepdims=True)
            self.p[slot][...] = p.astype(BF16)
            self.a[slot][...] = alpha
            self.m[...] = m_new

        def pv(self, kb, slot):
            v = vt_ref[kb, self.g * HEAD:(self.g + 1) * HEAD, :]
            upd = jnp.dot(v, self.p[slot][...], preferred_element_type=F32)
            self.acc[...] = self.a[slot][...] * self.acc[...] + upd

        def finish(self, cur):
            self.softmax(cur, True)
            self.pv(jnp.maximum(i - 1, 0), 1 - cur)
            self.pv(i, cur)
            out = self.acc[...] * (1.0 / self.l[...])
            o_ref[:, self.g * HEAD:(self.g + 1) * HEAD] = out.T.astype(BF16)

    heads = [Head(g) for g in range(ATTN_HEADS)]

    def iteration(kb, cur):
        for hd in heads:
            hd.scores(kb + 1, 1 - cur)
            hd.softmax(cur, False)
            hd.pv(jnp.maximum(kb - 1, 0), 1 - cur)

    for hd in heads:
        hd.init()
        hd.scores(0, 0)

    def pair(kk, carry):
        iteration(2 * kk, 0)
        iteration(2 * kk + 1, 1)
        return carry

    lax.fori_loop(0, i // 2, pair, 0)

    @pl.when(i % 2 == 1)
    def _():
        iteration(i - 1, 0)
        for hd in heads:
            hd.finish(1)

    @pl.when(i % 2 == 0)
    def _():
        for hd in heads:
            hd.finish(0)


def _attention(q1, q1_off, q2, k1, k1_off, k2, k2_shared, vt, chunk_causal):
    B, S = q1.shape[0], q1.shape[1]
    t = min(TILE_ATTN, S)
    nq = S // t
    w = ATTN_HEADS * LANES
    assert vt.shape[1] == nq and vt.shape[3] == t
    assert q1_off % ATTN_HEADS == 0 and k1_off % ATTN_HEADS == 0
    qspec = lambda off: pl.BlockSpec((None, t, w), lambda b, g, i: (b, i, g + off // ATTN_HEADS))
    kspec = lambda off: pl.BlockSpec((None, S, w), lambda b, g, i: (b, 0, g + off // ATTN_HEADS))
    k2spec = (pl.BlockSpec((None, S, LANES), lambda b, g, i: (b, 0, 0))
              if k2_shared else kspec(0))
    head_scratch = [pltpu.VMEM((t, 2 * LANES), BF16),
                    pltpu.VMEM((t, t), F32), pltpu.VMEM((t, t), F32),
                    pltpu.VMEM((1, t), F32), pltpu.VMEM((1, t), F32),
                    pltpu.VMEM((t, t), BF16), pltpu.VMEM((t, t), BF16),
                    pltpu.VMEM((1, t), F32), pltpu.VMEM((1, t), F32),
                    pltpu.VMEM((1, t), F32), pltpu.VMEM((1, t), F32),
                    pltpu.VMEM((HEAD, t), F32)]
    return pl.pallas_call(
        functools.partial(_attn_kernel, chunk_causal=chunk_causal, k2_shared=k2_shared),
        grid=(B, N_HEADS // ATTN_HEADS, nq),
        in_specs=[qspec(q1_off), qspec(0), kspec(k1_off), k2spec,
                  pl.BlockSpec((None, nq, ATTN_HEADS * HEAD, t), lambda b, g, i: (b, 0, g, 0))],
        out_specs=pl.BlockSpec((None, t, ATTN_HEADS * HEAD), lambda b, g, i: (b, i, g)),
        out_shape=jax.ShapeDtypeStruct((B, S, N_HEADS * HEAD), BF16),
        scratch_shapes=head_scratch * ATTN_HEADS,
        compiler_params=_params(("arbitrary", "arbitrary", "arbitrary")),
    )(q1, q2, k1, k2, vt)


def _mix_kernel(om_ref, of_ref, gm_ref, gf_ref, x_ref, wm_ref, wf_ref, wo_ref, h_ref):
    y_m = jnp.dot(om_ref[...], wm_ref[...], preferred_element_type=F32)
    y_f = jnp.dot(of_ref[...], wf_ref[...], preferred_element_type=F32)
    z = (gm_ref[...].astype(F32) * y_m + gf_ref[...].astype(F32) * y_f).astype(BF16)
    h_ref[...] = x_ref[...] + jnp.dot(z, wo_ref[...], preferred_element_type=F32)


def _mix(o_mla, o_fox, proj, x, w_m, w_f, w_o):
    B, S, D = x.shape
    tm = min(TILE_MIX, S)
    hd = o_mla.shape[2]
    row = lambda w, c=0: pl.BlockSpec((None, tm, w), lambda b, i: (b, i, c))
    return pl.pallas_call(
        _mix_kernel,
        grid=(B, S // tm),
        in_specs=[row(hd), row(hd), row(D, 0), row(D, 1), row(D),
                  _resident(w_m.shape), _resident(w_f.shape), _resident(w_o.shape)],
        out_specs=row(D),
        out_shape=jax.ShapeDtypeStruct((B, S, D), F32),
        compiler_params=_params(("arbitrary", "arbitrary")),
    )(o_mla, o_fox, proj, proj, x, w_m, w_f, w_o)


def _mlp_kernel(h_ref, g_ref, wu_ref, wd_ref, fg_ref, o_ref, hn_ref, *, final_norm):
    j = pl.program_id(2)

    @pl.when(j == 0)
    def _():
        h = h_ref[...]
        hn_ref[...] = _rms(h, g_ref[...]).astype(BF16)
        o_ref[...] = h

    u = jnp.dot(hn_ref[...], wu_ref[...], preferred_element_type=F32)
    u = jnp.square(jnp.maximum(u, 0.0)).astype(BF16)
    o_ref[...] += jnp.dot(u, wd_ref[...], preferred_element_type=F32)

    if final_norm:
        @pl.when(j == pl.num_programs(2) - 1)
        def _():
            o_ref[...] = _rms(o_ref[...], fg_ref[...])


def _mlp(h, mlp_g, w_up, w_down, final_g, final_norm):
    B, S, D = h.shape
    Fdim = w_up.shape[1]
    tm = min(TILE_MLP_M, S)
    tf = TILE_MLP_F
    row = pl.BlockSpec((None, tm, D), lambda b, i, j: (b, i, 0))
    return pl.pallas_call(
        functools.partial(_mlp_kernel, final_norm=final_norm),
        grid=(B, S // tm, Fdim // tf),
        in_specs=[row, _resident(mlp_g.shape),
                  pl.BlockSpec((D, tf), lambda b, i, j: (0, j)),
                  pl.BlockSpec((tf, D), lambda b, i, j: (j, 0)),
                  _resident(final_g.shape)],
        out_specs=row,
        out_shape=jax.ShapeDtypeStruct((B, S, D), F32),
        scratch_shapes=[pltpu.VMEM((tm, D), BF16)],
        compiler_params=_params(("arbitrary", "arbitrary", "arbitrary")),
    )(h, mlp_g, w_up, w_down, final_g)


def _layer_weights(w_in, w_uq, w_ukv, fox_f_bias):
    hd = N_HEADS * HEAD
    o_ckv = Q_LORA
    o_kr = o_ckv + KV_LORA
    o_fq = o_kr + ROPE
    o_fk = o_fq + hd
    o_fv = o_fk + hd
    o_ff = o_fv + hd
    o_g = o_ff + N_HEADS
    half = ROPE // 2
    kr = w_in[:, o_kr:o_fq]
    kr4 = jnp.concatenate([kr, -kr[:, half:], kr[:, :half]], axis=1)
    ff = jnp.pad(w_in[:, o_ff:o_g], ((0, 0), (0, LANES - N_HEADS)))
    w_a = jnp.concatenate([w_in[:, :o_kr], kr4, ff], axis=1).astype(BF16)
    w_b = jnp.concatenate([w_in[:, o_g:], w_in[:, o_fq:o_fv]], axis=1).astype(BF16)
    w_fvt = w_in[:, o_fv:o_ff].T.astype(BF16)

    uq = w_uq.reshape(Q_LORA, N_HEADS, HEAD + ROPE)
    x1 = uq[:, :, HEAD:HEAD + half]
    x2 = uq[:, :, HEAD + half:]
    rope4 = jnp.concatenate([x1, x2, -x2, x1], axis=2).reshape(Q_LORA, hd)
    w_q = jnp.concatenate([uq[:, :, :HEAD].reshape(Q_LORA, hd), rope4], axis=1).astype(BF16)

    ukv = w_ukv.reshape(KV_LORA, N_HEADS, 2 * HEAD)
    w_k = ukv[:, :, :HEAD].reshape(KV_LORA, hd).astype(BF16)
    w_vt = ukv[:, :, HEAD:].reshape(KV_LORA, hd).T.astype(BF16)
    f_bias = jnp.pad(fox_f_bias, (0, LANES - N_HEADS)).reshape(1, LANES)
    return w_a, w_b, w_fvt, w_q, w_k, w_vt, f_bias


def _rope_tables(S):
    pos = jnp.arange(S, dtype=F32)
    inv = 1.0 / (ROPE_THETA ** (jnp.arange(0, ROPE, 2, dtype=F32) / ROPE))
    ang = pos[:, None] * inv[None, :]
    zeros = jnp.zeros((S, LANES - ROPE), F32)
    cos = jnp.concatenate([jnp.cos(ang), jnp.cos(ang), zeros], axis=1)
    sin = jnp.concatenate([jnp.sin(ang), jnp.sin(ang), zeros], axis=1)
    return cos, sin


def kernel(x, attn_norm, w_in, fox_f_bias, q_norm, w_uq, kv_norm, w_ukv, w_mla_branch,
           w_fox_branch, w_out, mlp_norm, w_up, w_down, final_norm):
    B, S, D = x.shape
    depth = w_in.shape[0]
    hd = N_HEADS * HEAD
    cos, sin = _rope_tables(S)
    mla_scale = LOG2E / math.sqrt(HEAD + ROPE)
    fox_scale = LOG2E / math.sqrt(HEAD)
    h = x
    for l in range(depth):
        w_a, w_b, w_fvt, w_q, w_k, w_vt, f_bias = _layer_weights(
            w_in[l], w_uq[l], w_ukv[l], fox_f_bias[l])
        (xn, q1, q2, k1, k2, vt, fvt, fq2, fk2) = _prep(
            h, attn_norm[l].reshape(1, D), w_a, q_norm[l].reshape(1, Q_LORA), w_q,
            kv_norm[l].reshape(1, KV_LORA), w_k, w_vt, w_fvt, f_bias, cos, sin, mla_scale)
        proj = _proj(xn, w_b, 2 * D, hd, fox_scale)
        o_mla = _attention(q1, 0, q2, k1, 0, k2, True, vt, chunk_causal=True)
        fq_blk = 2 * D // LANES
        o_fox = _attention(proj, fq_blk, fq2, proj, fq_blk + N_HEADS, fk2, False, fvt,
                           chunk_causal=False)
        h = _mix(o_mla, o_fox, proj, h, w_mla_branch[l].astype(BF16),
                 w_fox_branch[l].astype(BF16), w_out[l].astype(BF16))
        h = _mlp(h, mlp_norm[l].reshape(1, D), w_up[l].astype(BF16),
                 w_down[l].astype(BF16), final_norm.reshape(1, D),
                 final_norm=(l == depth - 1))
    return h
```

```python
import functools
import math

import jax
import jax.numpy as jnp
from jax import lax
from jax.experimental import pallas as pl
from jax.experimental.pallas import tpu as pltpu

F32 = jnp.float32
BF16 = jnp.bfloat16

EPS = 1e-6
LOG2E = math.log2(math.e)
ROPE_THETA = 10000.0
N_HEADS = 8
HEAD = 128
ROPE = 64
Q_LORA = 512
KV_LORA = 256
CHUNK = 64

LANES = 128
V7X_VMEM_LIMIT = 56 * 1024 * 1024

TILE_PREP = 512
TILE_ATTN = 512
V_ROWS = HEAD + 16
ATTN_HEADS = 2
TILE_PROJ_M = 1024
TILE_PROJ_N = 1024
TILE_MIX = 512
TILE_MLP_M = 1024
TILE_MLP_F = 512

_NT = (((1,), (1,)), ((), ()))


def _rms(v, g):
    return v * lax.rsqrt(jnp.mean(v * v, axis=-1, keepdims=True) + EPS) * g


def _params(sem, vmem=V7X_VMEM_LIMIT):
    return pltpu.CompilerParams(dimension_semantics=sem, vmem_limit_bytes=vmem)


def _resident(shape):
    nd = len(shape)
    return pl.BlockSpec(shape, lambda *_: (0,) * nd, pipeline_mode=pl.Buffered(1))


def _prep_kernel(x_ref, g_ref, wa_ref, qg_ref, wq_ref, kg_ref, wk_ref, wvt_ref,
                 wfvt_ref, fb_ref, cos_ref, sin_ref,
                 xn_ref, q1_ref, q2_ref, k1_ref, k2_ref, vt_ref, fvt_ref,
                 fq2_ref, fk2_ref, carry_ref, *, scale):
    tm = x_ref.shape[0]
    xn = _rms(x_ref[...], g_ref[...]).astype(BF16)
    xn_ref[...] = xn
    a = jnp.dot(xn, wa_ref[...], preferred_element_type=F32)
    cos = cos_ref[...]
    sin = sin_ref[...]

    ones_rows = (lax.broadcasted_iota(jnp.int32, (V_ROWS - HEAD, tm), 0) == 0).astype(BF16)

    def put_values(ref, vt):
        for h in range(N_HEADS):
            ref[h * V_ROWS:h * V_ROWS + HEAD, :] = vt[h * HEAD:(h + 1) * HEAD, :].astype(BF16)
            ref[h * V_ROWS + HEAD:(h + 1) * V_ROWS, :] = ones_rows

    def rope(r):
        return r * cos + pltpu.roll(r, ROPE, 1) * sin

    cqn = _rms(a[:, :Q_LORA], qg_ref[...]).astype(BF16)
    q = jnp.dot(cqn, wq_ref[...], preferred_element_type=F32)
    nq = N_HEADS * HEAD
    q1_ref[...] = (q[:, :nq] * scale).astype(BF16)
    for h in range(N_HEADS):
        r = q[:, nq + h * LANES: nq + (h + 1) * LANES]
        q2_ref[:, h * LANES:(h + 1) * LANES] = (rope(r) * scale).astype(BF16)

    ckvn = _rms(a[:, Q_LORA:Q_LORA + KV_LORA], kg_ref[...]).astype(BF16)
    k1_ref[...] = jnp.dot(ckvn, wk_ref[...], preferred_element_type=F32).astype(BF16)
    put_values(vt_ref, lax.dot_general(wvt_ref[...], ckvn, _NT, preferred_element_type=F32))
    off = Q_LORA + KV_LORA
    k2_ref[...] = rope(a[:, off:off + LANES]).astype(BF16)
    put_values(fvt_ref, lax.dot_general(wfvt_ref[...], xn, _NT, preferred_element_type=F32))

    z = a[:, off + LANES:off + 2 * LANES] + fb_ref[...]
    logf = jnp.minimum(z, 0.0) - jnp.log1p(jnp.exp(-jnp.abs(z)))

    @pl.when(pl.program_id(1) == 0)
    def _():
        carry_ref[...] = jnp.zeros_like(carry_ref)

    row = lax.broadcasted_iota(jnp.int32, (tm, tm), 0)
    col = lax.broadcasted_iota(jnp.int32, (tm, tm), 1)
    tri = (row >= col).astype(F32)
    cum = jnp.dot(tri, logf, preferred_element_type=F32,
                  precision=lax.Precision.HIGHEST) + carry_ref[...]
    carry_ref[...] = cum[tm - 1:tm, :]

    lane = lax.broadcasted_iota(jnp.int32, (tm, LANES), 1)
    for h in range(N_HEADS):
        c = cum[:, h:h + 1] * LOG2E
        hi = c.astype(BF16).astype(F32)
        r1 = c - hi
        mid = r1.astype(BF16).astype(F32)
        lo = r1 - mid
        qa = jnp.where(lane == 0, hi, jnp.where(lane == 1, mid, jnp.where(
            lane == 2, lo, jnp.where(lane < 6, 1.0, 0.0))))
        ka = jnp.where(lane < 3, 1.0, jnp.where(lane == 3, -hi, jnp.where(
            lane == 4, -mid, jnp.where(lane == 5, -lo, 0.0))))
        fq2_ref[:, h * LANES:(h + 1) * LANES] = qa.astype(BF16)
        fk2_ref[:, h * LANES:(h + 1) * LANES] = ka.astype(BF16)


def _prep(x, attn_g, w_a, q_g, w_q, kv_g, w_k, w_vt, w_fvt, f_bias, cos, sin, scale):
    B, S, D = x.shape
    tm = min(TILE_PREP, S)
    nb = S // tm
    hd = N_HEADS * HEAD
    row = lambda w: pl.BlockSpec((None, tm, w), lambda b, i: (b, i, 0))
    vt_spec = pl.BlockSpec((None, None, N_HEADS * V_ROWS, tm), lambda b, i: (b, i, 0, 0))
    tab = pl.BlockSpec((tm, LANES), lambda b, i: (i, 0))
    out_shape = [
        jax.ShapeDtypeStruct((B, S, D), BF16),
        jax.ShapeDtypeStruct((B, S, hd), BF16),
        jax.ShapeDtypeStruct((B, S, hd), BF16),
        jax.ShapeDtypeStruct((B, S, hd), BF16),
        jax.ShapeDtypeStruct((B, S, LANES), BF16),
        jax.ShapeDtypeStruct((B, nb, N_HEADS * V_ROWS, tm), BF16),
        jax.ShapeDtypeStruct((B, nb, N_HEADS * V_ROWS, tm), BF16),
        jax.ShapeDtypeStruct((B, S, hd), BF16),
        jax.ShapeDtypeStruct((B, S, hd), BF16),
    ]
    return pl.pallas_call(
        functools.partial(_prep_kernel, scale=scale),
        grid=(B, nb),
        in_specs=[row(D), _resident(attn_g.shape), _resident(w_a.shape),
                  _resident(q_g.shape), _resident(w_q.shape), _resident(kv_g.shape),
                  _resident(w_k.shape), _resident(w_vt.shape), _resident(w_fvt.shape),
                  _resident(f_bias.shape), tab, tab],
        out_specs=[row(D), row(hd), row(hd), row(hd), row(LANES), vt_spec, vt_spec,
                   row(hd), row(hd)],
        out_shape=out_shape,
        scratch_shapes=[pltpu.VMEM((1, LANES), F32)],
        compiler_params=_params(("arbitrary", "arbitrary")),
    )(x, attn_g, w_a, q_g, w_q, kv_g, w_k, w_vt, w_fvt, f_bias, cos, sin)


def _proj_kernel(xn_ref, w_ref, o_ref, *, gate, n_scaled, scale):
    acc = jnp.dot(xn_ref[...], w_ref[...], preferred_element_type=F32)
    if gate:
        o_ref[...] = jax.nn.sigmoid(acc).astype(BF16)
    else:
        fac = jnp.where(pl.program_id(2) < n_scaled, scale, 1.0).astype(F32)
        o_ref[...] = (acc * fac).astype(BF16)


def _proj(xn, w, gate, n_scaled_cols=0, scale=1.0):
    B, S, D = xn.shape
    N = w.shape[1]
    tm = min(TILE_PROJ_M, S)
    tn = TILE_PROJ_N
    return pl.pallas_call(
        functools.partial(_proj_kernel, gate=gate, n_scaled=n_scaled_cols // tn, scale=scale),
        grid=(B, S // tm, N // tn),
        in_specs=[pl.BlockSpec((None, tm, D), lambda b, i, j: (b, i, 0)),
                  pl.BlockSpec((D, tn), lambda b, i, j: (0, j))],
        out_specs=pl.BlockSpec((None, tm, tn), lambda b, i, j: (b, i, j)),
        out_shape=jax.ShapeDtypeStruct((B, S, N), BF16),
        compiler_params=_params(("arbitrary", "arbitrary", "arbitrary")),
    )(xn, w)


def _attn_kernel(q1_ref, q2_ref, k1_ref, k2_ref, vt_ref, o_ref, *scratch,
                 chunk_causal, k2_shared):
    t = q1_ref.shape[0]
    i = pl.program_id(2)
    per_head = len(scratch) // ATTN_HEADS

    def lanes(ref, g, rows=slice(None)):
        return ref[rows, g * LANES:(g + 1) * LANES]

    class Head:
        def __init__(self, g):
            (self.q, s0, s1, b0, b1, p0, p1, a0, a1, self.m,
             self.acc) = scratch[g * per_head:(g + 1) * per_head]
            self.g = g
            self.s = (s0, s1)
            self.b = (b0, b1)
            self.p = (p0, p1)
            self.a = (a0, a1)

        def init(self):
            qq = jnp.concatenate([lanes(q1_ref, self.g), lanes(q2_ref, self.g)], axis=1)
            self.q[...] = qq.astype(F32).T.astype(BF16)
            self.m[...] = jnp.full_like(self.m, -jnp.inf)
            self.acc[...] = jnp.zeros_like(self.acc)
            self.p[1][...] = jnp.zeros_like(self.p[1])
            self.a[1][...] = jnp.ones_like(self.a[1])

        def scores(self, kb, slot):
            rows = pl.ds(pl.multiple_of(kb * t, t), t)
            k2 = k2_ref[rows, :] if k2_shared else lanes(k2_ref, self.g, rows)
            k = jnp.concatenate([lanes(k1_ref, self.g, rows), k2], axis=1)
            s = jnp.dot(k, self.q[...], preferred_element_type=F32)
            self.s[slot][...] = s
            self.b[slot][...] = jnp.max(s, axis=0, keepdims=True)

        def softmax(self, slot, masked):
            s = self.s[slot][...]
            if masked:
                kj = lax.broadcasted_iota(jnp.int32, (t, t), 0)
                qi = lax.broadcasted_iota(jnp.int32, (t, t), 1)
                if chunk_causal:
                    shift = CHUNK.bit_length() - 1
                    valid = (kj >> shift) <= (qi >> shift)
                else:
                    valid = kj <= qi
                s = jnp.where(valid, s, -jnp.inf)
                blk_max = jnp.max(s, axis=0, keepdims=True)
            else:
                blk_max = self.b[slot][...]
            m_prev = self.m[...]
            m_new = jnp.maximum(m_prev, blk_max)
            alpha = jnp.exp2(m_prev - m_new)
            p = jnp.exp2(s - m_new)
            self.p[slot][...] = p.astype(BF16)
            self.a[slot][...] = alpha
            self.m[...] = m_new

        def pv(self, kb, slot):
            v = vt_ref[kb, self.g * V_ROWS:(self.g + 1) * V_ROWS, :]
            upd = jnp.dot(v, self.p[slot][...], preferred_element_type=F32)
            self.acc[...] = self.a[slot][...] * self.acc[...] + upd

        def finish(self, cur):
            self.softmax(cur, True)
            self.pv(jnp.maximum(i - 1, 0), 1 - cur)
            self.pv(i, cur)
            out = self.acc[:HEAD, :] * (1.0 / self.acc[HEAD:HEAD + 1, :])
            o_ref[:, self.g * HEAD:(self.g + 1) * HEAD] = out.T.astype(BF16)

    heads = [Head(g) for g in range(ATTN_HEADS)]

    def iteration(kb, cur):
        for hd in heads:
            hd.scores(kb + 1, 1 - cur)
            hd.softmax(cur, False)
            hd.pv(jnp.maximum(kb - 1, 0), 1 - cur)

    for hd in heads:
        hd.init()
        hd.scores(0, 0)

    def pair(kk, carry):
        iteration(2 * kk, 0)
        iteration(2 * kk + 1, 1)
        return carry

    lax.fori_loop(0, i // 2, pair, 0)

    @pl.when(i % 2 == 1)
    def _():
        iteration(i - 1, 0)
        for hd in heads:
            hd.finish(1)

    @pl.when(i % 2 == 0)
    def _():
        for hd in heads:
            hd.finish(0)


def _attention(q1, q1_off, q2, k1, k1_off, k2, k2_shared, vt, chunk_causal):
    B, S = q1.shape[0], q1.shape[1]
    t = min(TILE_ATTN, S)
    nq = S // t
    w = ATTN_HEADS * LANES
    assert vt.shape[1] == nq and vt.shape[3] == t
    assert q1_off % ATTN_HEADS == 0 and k1_off % ATTN_HEADS == 0
    qspec = lambda off: pl.BlockSpec((None, t, w), lambda b, g, i: (b, i, g + off // ATTN_HEADS))
    kspec = lambda off: pl.BlockSpec((None, S, w), lambda b, g, i: (b, 0, g + off // ATTN_HEADS))
    k2spec = (pl.BlockSpec((None, S, LANES), lambda b, g, i: (b, 0, 0))
              if k2_shared else kspec(0))
    head_scratch = [pltpu.VMEM((2 * LANES, t), BF16),
                    pltpu.VMEM((t, t), F32), pltpu.VMEM((t, t), F32),
                    pltpu.VMEM((1, t), F32), pltpu.VMEM((1, t), F32),
                    pltpu.VMEM((t, t), BF16), pltpu.VMEM((t, t), BF16),
                    pltpu.VMEM((1, t), F32), pltpu.VMEM((1, t), F32),
                    pltpu.VMEM((1, t), F32),
                    pltpu.VMEM((V_ROWS, t), F32)]
    return pl.pallas_call(
        functools.partial(_attn_kernel, chunk_causal=chunk_causal, k2_shared=k2_shared),
        grid=(B, N_HEADS // ATTN_HEADS, nq),
        in_specs=[qspec(q1_off), qspec(0), kspec(k1_off), k2spec,
                  pl.BlockSpec((None, nq, ATTN_HEADS * V_ROWS, t), lambda b, g, i: (b, 0, g, 0))],
        out_specs=pl.BlockSpec((None, t, ATTN_HEADS * HEAD), lambda b, g, i: (b, i, g)),
        out_shape=jax.ShapeDtypeStruct((B, S, N_HEADS * HEAD), BF16),
        scratch_shapes=head_scratch * ATTN_HEADS,
        compiler_params=_params(("arbitrary", "arbitrary", "arbitrary")),
    )(q1, q2, k1, k2, vt)


def _mix_kernel(om_ref, of_ref, gm_ref, gf_ref, x_ref, wm_ref, wf_ref, wo_ref, h_ref):
    y_m = jnp.dot(om_ref[...], wm_ref[...], preferred_element_type=F32)
    y_f = jnp.dot(of_ref[...], wf_ref[...], preferred_element_type=F32)
    z = (gm_ref[...].astype(F32) * y_m + gf_ref[...].astype(F32) * y_f).astype(BF16)
    h_ref[...] = x_ref[...] + jnp.dot(z, wo_ref[...], preferred_element_type=F32)


def _mix(o_mla, o_fox, proj, x, w_m, w_f, w_o):
    B, S, D = x.shape
    tm = min(TILE_MIX, S)
    hd = o_mla.shape[2]
    row = lambda w, c=0: pl.BlockSpec((None, tm, w), lambda b, i: (b, i, c))
    return pl.pallas_call(
        _mix_kernel,
        grid=(B, S // tm),
        in_specs=[row(hd), row(hd), row(D, 0), row(D, 1), row(D),
                  _resident(w_m.shape), _resident(w_f.shape), _resident(w_o.shape)],
        out_specs=row(D),
        out_shape=jax.ShapeDtypeStruct((B, S, D), F32),
        compiler_params=_params(("arbitrary", "arbitrary")),
    )(o_mla, o_fox, proj, proj, x, w_m, w_f, w_o)


def _mlp_kernel(h_ref, g_ref, wu_ref, wd_ref, fg_ref, o_ref, hn_ref, *, final_norm):
    j = pl.program_id(2)

    @pl.when(j == 0)
    def _():
        h = h_ref[...]
        hn_ref[...] = _rms(h, g_ref[...]).astype(BF16)
        o_ref[...] = h

    u = jnp.dot(hn_ref[...], wu_ref[...], preferred_element_type=F32)
    u = jnp.square(jnp.maximum(u, 0.0)).astype(BF16)
    o_ref[...] += jnp.dot(u, wd_ref[...], preferred_element_type=F32)

    if final_norm:
        @pl.when(j == pl.num_programs(2) - 1)
        def _():
            o_ref[...] = _rms(o_ref[...], fg_ref[...])


def _mlp(h, mlp_g, w_up, w_down, final_g, final_norm):
    B, S, D = h.shape
    Fdim = w_up.shape[1]
    tm = min(TILE_MLP_M, S)
    tf = TILE_MLP_F
    row = pl.BlockSpec((None, tm, D), lambda b, i, j: (b, i, 0))
    return pl.pallas_call(
        functools.partial(_mlp_kernel, final_norm=final_norm),
        grid=(B, S // tm, Fdim // tf),
        in_specs=[row, _resident(mlp_g.shape),
                  pl.BlockSpec((D, tf), lambda b, i, j: (0, j)),
                  pl.BlockSpec((tf, D), lambda b, i, j: (j, 0)),
                  _resident(final_g.shape)],
        out_specs=row,
        out_shape=jax.ShapeDtypeStruct((B, S, D), F32),
        scratch_shapes=[pltpu.VMEM((tm, D), BF16)],
        compiler_params=_params(("arbitrary", "arbitrary", "arbitrary")),
    )(h, mlp_g, w_up, w_down, final_g)


def _layer_weights(w_in, w_uq, w_ukv, fox_f_bias):
    hd = N_HEADS * HEAD
    o_ckv = Q_LORA
    o_kr = o_ckv + KV_LORA
    o_fq = o_kr + ROPE
    o_fk = o_fq + hd
    o_fv = o_fk + hd
    o_ff = o_fv + hd
    o_g = o_ff + N_HEADS
    half = ROPE // 2
    kr = w_in[:, o_kr:o_fq]
    kr4 = jnp.concatenate([kr, -kr[:, half:], kr[:, :half]], axis=1)
    ff = jnp.pad(w_in[:, o_ff:o_g], ((0, 0), (0, LANES - N_HEADS)))
    w_a = jnp.concatenate([w_in[:, :o_kr], kr4, ff], axis=1).astype(BF16)
    w_g = w_in[:, o_g:].astype(BF16)
    w_fqk = w_in[:, o_fq:o_fv].astype(BF16)
    w_fvt = w_in[:, o_fv:o_ff].T.astype(BF16)

    uq = w_uq.reshape(Q_LORA, N_HEADS, HEAD + ROPE)
    x1 = uq[:, :, HEAD:HEAD + half]
    x2 = uq[:, :, HEAD + half:]
    rope4 = jnp.concatenate([x1, x2, -x2, x1], axis=2).reshape(Q_LORA, hd)
    w_q = jnp.concatenate([uq[:, :, :HEAD].reshape(Q_LORA, hd), rope4], axis=1).astype(BF16)

    ukv = w_ukv.reshape(KV_LORA, N_HEADS, 2 * HEAD)
    w_k = ukv[:, :, :HEAD].reshape(KV_LORA, hd).astype(BF16)
    w_vt = ukv[:, :, HEAD:].reshape(KV_LORA, hd).T.astype(BF16)
    f_bias = jnp.pad(fox_f_bias, (0, LANES - N_HEADS)).reshape(1, LANES)
    return w_a, w_g, w_fqk, w_fvt, w_q, w_k, w_vt, f_bias


def _rope_tables(S):
    pos = jnp.arange(S, dtype=F32)
    inv = 1.0 / (ROPE_THETA ** (jnp.arange(0, ROPE, 2, dtype=F32) / ROPE))
    ang = pos[:, None] * inv[None, :]
    zeros = jnp.zeros((S, LANES - ROPE), F32)
    cos = jnp.concatenate([jnp.cos(ang), jnp.cos(ang), zeros], axis=1)
    sin = jnp.concatenate([jnp.sin(ang), jnp.sin(ang), zeros], axis=1)
    return cos, sin


def kernel(x, attn_norm, w_in, fox_f_bias, q_norm, w_uq, kv_norm, w_ukv, w_mla_branch,
           w_fox_branch, w_out, mlp_norm, w_up, w_down, final_norm):
    B, S, D = x.shape
    depth = w_in.shape[0]
    hd = N_HEADS * HEAD
    cos, sin = _rope_tables(S)
    mla_scale = LOG2E / math.sqrt(HEAD + ROPE)
    fox_scale = LOG2E / math.sqrt(HEAD)
    h = x
    for l in range(depth):
        w_a, w_g, w_fqk, w_fvt, w_q, w_k, w_vt, f_bias = _layer_weights(
            w_in[l], w_uq[l], w_ukv[l], fox_f_bias[l])
        (xn, q1, q2, k1, k2, vt, fvt, fq2, fk2) = _prep(
            h, attn_norm[l].reshape(1, D), w_a, q_norm[l].reshape(1, Q_LORA), w_q,
            kv_norm[l].reshape(1, KV_LORA), w_k, w_vt, w_fvt, f_bias, cos, sin, mla_scale)
        gates = _proj(xn, w_g, gate=True)
        fqk = _proj(xn, w_fqk, gate=False, n_scaled_cols=hd, scale=fox_scale)
        o_mla = _attention(q1, 0, q2, k1, 0, k2, True, vt, chunk_causal=True)
        o_fox = _attention(fqk, 0, fq2, fqk, N_HEADS, fk2, False, fvt, chunk_causal=False)
        h = _mix(o_mla, o_fox, gates, h, w_mla_branch[l].astype(BF16),
                 w_fox_branch[l].astype(BF16), w_out[l].astype(BF16))
        h = _mlp(h, mlp_norm[l].reshape(1, D), w_up[l].astype(BF16),
                 w_down[l].astype(BF16), final_norm.reshape(1, D),
                 final_norm=(l == depth - 1))
    return h
```

```python
import functools
import math

import jax
import jax.numpy as jnp
from jax import lax
from jax.experimental import pallas as pl
from jax.experimental.pallas import tpu as pltpu

F32 = jnp.float32
BF16 = jnp.bfloat16

EPS = 1e-6
LOG2E = math.log2(math.e)
ROPE_THETA = 10000.0
N_HEADS = 8
HEAD = 128
ROPE = 64
Q_LORA = 512
KV_LORA = 256
CHUNK = 64

LANES = 128
V7X_VMEM_LIMIT = 56 * 1024 * 1024

TILE_PREP = 512
TILE_ATTN = 512
V_ROWS = HEAD + 16
ATTN_HEADS = 2
TILE_PROJ_M = 1024
TILE_PROJ_N = 1024
TILE_GATE_N = 2048
TILE_MIX = 512
TILE_MLP_M = 1024
TILE_MLP_F = 512

_NT = (((1,), (1,)), ((), ()))


def _rms(v, g):
    return v * lax.rsqrt(jnp.mean(v * v, axis=-1, keepdims=True) + EPS) * g


def _params(sem, vmem=V7X_VMEM_LIMIT):
    return pltpu.CompilerParams(dimension_semantics=sem, vmem_limit_bytes=vmem)


def _resident(shape):
    nd = len(shape)
    return pl.BlockSpec(shape, lambda *_: (0,) * nd, pipeline_mode=pl.Buffered(1))


def _prep_kernel(x_ref, g_ref, wa_ref, qg_ref, wq_ref, kg_ref, wk_ref, wvt_ref,
                 wfvt_ref, fb_ref, cos_ref, sin_ref,
                 xn_ref, q1_ref, q2_ref, k1_ref, k2_ref, vt_ref, fvt_ref,
                 fq2_ref, fk2_ref, carry_ref, *, scale):
    tm = x_ref.shape[0]
    xn = _rms(x_ref[...], g_ref[...]).astype(BF16)
    xn_ref[...] = xn
    a = jnp.dot(xn, wa_ref[...], preferred_element_type=F32)
    cos = cos_ref[...]
    sin = sin_ref[...]

    ones_rows = (lax.broadcasted_iota(jnp.int32, (V_ROWS - HEAD, tm), 0) == 0).astype(BF16)

    def put_values(ref, vt):
        for h in range(N_HEADS):
            ref[h * V_ROWS:h * V_ROWS + HEAD, :] = vt[h * HEAD:(h + 1) * HEAD, :].astype(BF16)
            ref[h * V_ROWS + HEAD:(h + 1) * V_ROWS, :] = ones_rows

    def rope(r):
        return r * cos + pltpu.roll(r, ROPE, 1) * sin

    cqn = _rms(a[:, :Q_LORA], qg_ref[...]).astype(BF16)
    q = jnp.dot(cqn, wq_ref[...], preferred_element_type=F32)
    nq = N_HEADS * HEAD
    q1_ref[...] = (q[:, :nq] * scale).astype(BF16)
    for h in range(N_HEADS):
        r = q[:, nq + h * LANES: nq + (h + 1) * LANES]
        q2_ref[:, h * LANES:(h + 1) * LANES] = (rope(r) * scale).astype(BF16)

    ckvn = _rms(a[:, Q_LORA:Q_LORA + KV_LORA], kg_ref[...]).astype(BF16)
    k1_ref[...] = jnp.dot(ckvn, wk_ref[...], preferred_element_type=F32).astype(BF16)
    put_values(vt_ref, lax.dot_general(wvt_ref[...], ckvn, _NT, preferred_element_type=F32))
    off = Q_LORA + KV_LORA
    k2_ref[...] = rope(a[:, off:off + LANES]).astype(BF16)
    put_values(fvt_ref, lax.dot_general(wfvt_ref[...], xn, _NT, preferred_element_type=F32))

    z = a[:, off + LANES:off + 2 * LANES] + fb_ref[...]
    logf = jnp.minimum(z, 0.0) - jnp.log1p(jnp.exp(-jnp.abs(z)))

    @pl.when(pl.program_id(1) == 0)
    def _():
        carry_ref[...] = jnp.zeros_like(carry_ref)

    row = lax.broadcasted_iota(jnp.int32, (tm, tm), 0)
    col = lax.broadcasted_iota(jnp.int32, (tm, tm), 1)
    tri = (row >= col).astype(F32)
    cum = jnp.dot(tri, logf, preferred_element_type=F32,
                  precision=lax.Precision.HIGHEST) + carry_ref[...]
    carry_ref[...] = cum[tm - 1:tm, :]

    lane = lax.broadcasted_iota(jnp.int32, (tm, LANES), 1)
    for h in range(N_HEADS):
        c = cum[:, h:h + 1] * LOG2E
        hi = c.astype(BF16).astype(F32)
        r1 = c - hi
        mid = r1.astype(BF16).astype(F32)
        lo = r1 - mid
        qa = jnp.where(lane == 0, hi, jnp.where(lane == 1, mid, jnp.where(
            lane == 2, lo, jnp.where(lane < 6, 1.0, 0.0))))
        ka = jnp.where(lane < 3, 1.0, jnp.where(lane == 3, -hi, jnp.where(
            lane == 4, -mid, jnp.where(lane == 5, -lo, 0.0))))
        fq2_ref[:, h * LANES:(h + 1) * LANES] = qa.astype(BF16)
        fk2_ref[:, h * LANES:(h + 1) * LANES] = ka.astype(BF16)


def _prep(x, attn_g, w_a, q_g, w_q, kv_g, w_k, w_vt, w_fvt, f_bias, cos, sin, scale):
    B, S, D = x.shape
    tm = min(TILE_PREP, S)
    nb = S // tm
    hd = N_HEADS * HEAD
    row = lambda w: pl.BlockSpec((None, tm, w), lambda b, i: (b, i, 0))
    vt_spec = pl.BlockSpec((None, None, N_HEADS * V_ROWS, tm), lambda b, i: (b, i, 0, 0))
    tab = pl.BlockSpec((tm, LANES), lambda b, i: (i, 0))
    out_shape = [
        jax.ShapeDtypeStruct((B, S, D), BF16),
        jax.ShapeDtypeStruct((B, S, hd), BF16),
        jax.ShapeDtypeStruct((B, S, hd), BF16),
        jax.ShapeDtypeStruct((B, S, hd), BF16),
        jax.ShapeDtypeStruct((B, S, LANES), BF16),
        jax.ShapeDtypeStruct((B, nb, N_HEADS * V_ROWS, tm), BF16),
        jax.ShapeDtypeStruct((B, nb, N_HEADS * V_ROWS, tm), BF16),
        jax.ShapeDtypeStruct((B, S, hd), BF16),
        jax.ShapeDtypeStruct((B, S, hd), BF16),
    ]
    return pl.pallas_call(
        functools.partial(_prep_kernel, scale=scale),
        grid=(B, nb),
        in_specs=[row(D), _resident(attn_g.shape), _resident(w_a.shape),
                  _resident(q_g.shape), _resident(w_q.shape), _resident(kv_g.shape),
                  _resident(w_k.shape), _resident(w_vt.shape), _resident(w_fvt.shape),
                  _resident(f_bias.shape), tab, tab],
        out_specs=[row(D), row(hd), row(hd), row(hd), row(LANES), vt_spec, vt_spec,
                   row(hd), row(hd)],
        out_shape=out_shape,
        scratch_shapes=[pltpu.VMEM((1, LANES), F32)],
        compiler_params=_params(("arbitrary", "arbitrary")),
    )(x, attn_g, w_a, q_g, w_q, kv_g, w_k, w_vt, w_fvt, f_bias, cos, sin)


def _proj_kernel(xn_ref, w_ref, o_ref, *, gate, n_scaled, scale):
    acc = jnp.dot(xn_ref[...], w_ref[...], preferred_element_type=F32)
    if gate:
        o_ref[...] = jax.nn.sigmoid(acc).astype(BF16)
    else:
        fac = jnp.where(pl.program_id(2) < n_scaled, scale, 1.0).astype(F32)
        o_ref[...] = (acc * fac).astype(BF16)


def _proj(xn, w, gate, n_scaled_cols=0, scale=1.0):
    B, S, D = xn.shape
    N = w.shape[1]
    tm = min(TILE_PROJ_M, S)
    tn = TILE_GATE_N if gate else TILE_PROJ_N
    return pl.pallas_call(
        functools.partial(_proj_kernel, gate=gate, n_scaled=n_scaled_cols // tn, scale=scale),
        grid=(B, S // tm, N // tn),
        in_specs=[pl.BlockSpec((None, tm, D), lambda b, i, j: (b, i, 0)),
                  pl.BlockSpec((D, tn), lambda b, i, j: (0, j))],
        out_specs=pl.BlockSpec((None, tm, tn), lambda b, i, j: (b, i, j)),
        out_shape=jax.ShapeDtypeStruct((B, S, N), BF16),
        compiler_params=_params(("arbitrary", "arbitrary", "arbitrary")),
    )(xn, w)


def _attn_kernel(q1_ref, q2_ref, k1_ref, k2_ref, vt_ref, o_ref, *scratch,
                 chunk_causal, k2_shared):
    t = q1_ref.shape[0]
    i = pl.program_id(2)
    per_head = len(scratch) // ATTN_HEADS

    def lanes(ref, g, rows=slice(None)):
        return ref[rows, g * LANES:(g + 1) * LANES]

    class Head:
        def __init__(self, g):
            self.q, s0, s1, b0, b1, self.m, self.acc = scratch[g * per_head:(g + 1) * per_head]
            self.g = g
            self.s = (s0, s1)
            self.b = (b0, b1)

        def init(self):
            qq = jnp.concatenate([lanes(q1_ref, self.g), lanes(q2_ref, self.g)], axis=1)
            self.q[...] = qq.astype(F32).T.astype(BF16)
            self.m[...] = jnp.full_like(self.m, -jnp.inf)
            self.acc[...] = jnp.zeros_like(self.acc)

        def scores(self, kb, slot):
            rows = pl.ds(pl.multiple_of(kb * t, t), t)
            k2 = k2_ref[rows, :] if k2_shared else lanes(k2_ref, self.g, rows)
            k = jnp.concatenate([lanes(k1_ref, self.g, rows), k2], axis=1)
            s = jnp.dot(k, self.q[...], preferred_element_type=F32)
            self.s[slot][...] = s
            self.b[slot][...] = jnp.max(s, axis=0, keepdims=True)

        def softmax(self, slot, masked):
            s = self.s[slot][...]
            if masked:
                kj = lax.broadcasted_iota(jnp.int32, (t, t), 0)
                qi = lax.broadcasted_iota(jnp.int32, (t, t), 1)
                if chunk_causal:
                    shift = CHUNK.bit_length() - 1
                    valid = (kj >> shift) <= (qi >> shift)
                else:
                    valid = kj <= qi
                s = jnp.where(valid, s, -jnp.inf)
                blk_max = jnp.max(s, axis=0, keepdims=True)
            else:
                blk_max = self.b[slot][...]
            m_prev = self.m[...]
            m_new = jnp.maximum(m_prev, blk_max)
            alpha = jnp.exp2(m_prev - m_new)
            p = jnp.exp2(s - m_new)
            self.m[...] = m_new
            return p.astype(BF16), alpha

        def pv(self, kb, p, alpha):
            v = vt_ref[kb, self.g * V_ROWS:(self.g + 1) * V_ROWS, :]
            upd = jnp.dot(v, p, preferred_element_type=F32)
            self.acc[...] = alpha * self.acc[...] + upd

        def finish(self, cur):
            self.pv(i, *self.softmax(cur, True))
            out = self.acc[:HEAD, :] * (1.0 / self.acc[HEAD:HEAD + 1, :])
            o_ref[:, self.g * HEAD:(self.g + 1) * HEAD] = out.T.astype(BF16)

    heads = [Head(g) for g in range(ATTN_HEADS)]

    def iteration(kb, cur):
        for hd in heads:
            hd.scores(kb + 1, 1 - cur)
            hd.pv(kb, *hd.softmax(cur, False))

    for hd in heads:
        hd.init()
        hd.scores(0, 0)

    def pair(kk, carry):
        iteration(2 * kk, 0)
        iteration(2 * kk + 1, 1)
        return carry

    lax.fori_loop(0, i // 2, pair, 0)

    @pl.when(i % 2 == 1)
    def _():
        iteration(i - 1, 0)
        for hd in heads:
            hd.finish(1)

    @pl.when(i % 2 == 0)
    def _():
        for hd in heads:
            hd.finish(0)


def _attention(q1, q1_off, q2, k1, k1_off, k2, k2_shared, vt, chunk_causal):
    B, S = q1.shape[0], q1.shape[1]
    t = min(TILE_ATTN, S)
    nq = S // t
    w = ATTN_HEADS * LANES
    assert vt.shape[1] == nq and vt.shape[3] == t
    assert q1_off % ATTN_HEADS == 0 and k1_off % ATTN_HEADS == 0
    qspec = lambda off: pl.BlockSpec((None, t, w), lambda b, g, i: (b, i, g + off // ATTN_HEADS))
    kspec = lambda off: pl.BlockSpec((None, S, w), lambda b, g, i: (b, 0, g + off // ATTN_HEADS))
    k2spec = (pl.BlockSpec((None, S, LANES), lambda b, g, i: (b, 0, 0))
              if k2_shared else kspec(0))
    head_scratch = [pltpu.VMEM((2 * LANES, t), BF16),
                    pltpu.VMEM((t, t), F32), pltpu.VMEM((t, t), F32),
                    pltpu.VMEM((1, t), F32), pltpu.VMEM((1, t), F32),
                    pltpu.VMEM((1, t), F32),
                    pltpu.VMEM((V_ROWS, t), F32)]
    return pl.pallas_call(
        functools.partial(_attn_kernel, chunk_causal=chunk_causal, k2_shared=k2_shared),
        grid=(B, N_HEADS // ATTN_HEADS, nq),
        in_specs=[qspec(q1_off), qspec(0), kspec(k1_off), k2spec,
                  pl.BlockSpec((None, nq, ATTN_HEADS * V_ROWS, t), lambda b, g, i: (b, 0, g, 0))],
        out_specs=pl.BlockSpec((None, t, ATTN_HEADS * HEAD), lambda b, g, i: (b, i, g)),
        out_shape=jax.ShapeDtypeStruct((B, S, N_HEADS * HEAD), BF16),
        scratch_shapes=head_scratch * ATTN_HEADS,
        compiler_params=_params(("arbitrary", "arbitrary", "arbitrary")),
    )(q1, q2, k1, k2, vt)


def _mix_kernel(om_ref, of_ref, gm_ref, gf_ref, x_ref, wm_ref, wf_ref, wo_ref, h_ref):
    y_m = jnp.dot(om_ref[...], wm_ref[...], preferred_element_type=F32)
    y_f = jnp.dot(of_ref[...], wf_ref[...], preferred_element_type=F32)
    z = (gm_ref[...].astype(F32) * y_m + gf_ref[...].astype(F32) * y_f).astype(BF16)
    h_ref[...] = x_ref[...] + jnp.dot(z, wo_ref[...], preferred_element_type=F32)


def _mix(o_mla, o_fox, proj, x, w_m, w_f, w_o):
    B, S, D = x.shape
    tm = min(TILE_MIX, S)
    hd = o_mla.shape[2]
    row = lambda w, c=0: pl.BlockSpec((None, tm, w), lambda b, i: (b, i, c))
    return pl.pallas_call(
        _mix_kernel,
        grid=(B, S // tm),
        in_specs=[row(hd), row(hd), row(D, 0), row(D, 1), row(D),
                  _resident(w_m.shape), _resident(w_f.shape), _resident(w_o.shape)],
        out_specs=row(D),
        out_shape=jax.ShapeDtypeStruct((B, S, D), F32),
        compiler_params=_params(("arbitrary", "arbitrary")),
    )(o_mla, o_fox, proj, proj, x, w_m, w_f, w_o)


def _mlp_kernel(h_ref, g_ref, wu_ref, wd_ref, fg_ref, o_ref, hn_ref, *, final_norm):
    j = pl.program_id(2)

    def step(first):
        if first:
            hn_ref[...] = _rms(h_ref[...], g_ref[...]).astype(BF16)
        u = jnp.dot(hn_ref[...], wu_ref[...], preferred_element_type=F32)
        u = jnp.square(jnp.maximum(u, 0.0)).astype(BF16)
        d = jnp.dot(u, wd_ref[...], preferred_element_type=F32)
        o_ref[...] = (h_ref[...] if first else o_ref[...]) + d

    pl.when(j == 0)(lambda: step(True))
    pl.when(j > 0)(lambda: step(False))

    if final_norm:
        @pl.when(j == pl.num_programs(2) - 1)
        def _():
            o_ref[...] = _rms(o_ref[...], fg_ref[...])


def _mlp(h, mlp_g, w_up, w_down, final_g, final_norm):
    B, S, D = h.shape
    Fdim = w_up.shape[1]
    tm = min(TILE_MLP_M, S)
    tf = TILE_MLP_F
    row = pl.BlockSpec((None, tm, D), lambda b, i, j: (b, i, 0))
    return pl.pallas_call(
        functools.partial(_mlp_kernel, final_norm=final_norm),
        grid=(B, S // tm, Fdim // tf),
        in_specs=[row, _resident(mlp_g.shape),
                  pl.BlockSpec((D, tf), lambda b, i, j: (0, j)),
                  pl.BlockSpec((tf, D), lambda b, i, j: (j, 0)),
                  _resident(final_g.shape)],
        out_specs=row,
        out_shape=jax.ShapeDtypeStruct((B, S, D), F32),
        scratch_shapes=[pltpu.VMEM((tm, D), BF16)],
        compiler_params=_params(("arbitrary", "arbitrary", "arbitrary")),
    )(h, mlp_g, w_up, w_down, final_g)


def _layer_weights(w_in, w_uq, w_ukv, fox_f_bias):
    hd = N_HEADS * HEAD
    o_ckv = Q_LORA
    o_kr = o_ckv + KV_LORA
    o_fq = o_kr + ROPE
    o_fk = o_fq + hd
    o_fv = o_fk + hd
    o_ff = o_fv + hd
    o_g = o_ff + N_HEADS
    half = ROPE // 2
    w_in = w_in.astype(BF16)
    kr = w_in[:, o_kr:o_fq]
    kr4 = jnp.concatenate([kr, -kr[:, half:], kr[:, :half]], axis=1)
    ff = jnp.pad(w_in[:, o_ff:o_g], ((0, 0), (0, LANES - N_HEADS)))
    w_a = jnp.concatenate([w_in[:, :o_kr], kr4, ff], axis=1)
    w_g = w_in[:, o_g:]
    w_fqk = w_in[:, o_fq:o_fv]
    w_fvt = w_in[:, o_fv:o_ff].T

    uq = w_uq.reshape(Q_LORA, N_HEADS, HEAD + ROPE)
    x1 = uq[:, :, HEAD:HEAD + half]
    x2 = uq[:, :, HEAD + half:]
    rope4 = jnp.concatenate([x1, x2, -x2, x1], axis=2).reshape(Q_LORA, hd)
    w_q = jnp.concatenate([uq[:, :, :HEAD].reshape(Q_LORA, hd), rope4], axis=1).astype(BF16)

    ukv = w_ukv.reshape(KV_LORA, N_HEADS, 2 * HEAD)
    w_k = ukv[:, :, :HEAD].reshape(KV_LORA, hd).astype(BF16)
    w_vt = ukv[:, :, HEAD:].reshape(KV_LORA, hd).T.astype(BF16)
    f_bias = jnp.pad(fox_f_bias, (0, LANES - N_HEADS)).reshape(1, LANES)
    return w_a, w_g, w_fqk, w_fvt, w_q, w_k, w_vt, f_bias


def _rope_tables(S):
    pos = jnp.arange(S, dtype=F32)
    inv = 1.0 / (ROPE_THETA ** (jnp.arange(0, ROPE, 2, dtype=F32) / ROPE))
    ang = pos[:, None] * inv[None, :]
    zeros = jnp.zeros((S, LANES - ROPE), F32)
    cos = jnp.concatenate([jnp.cos(ang), jnp.cos(ang), zeros], axis=1)
    sin = jnp.concatenate([jnp.sin(ang), jnp.sin(ang), zeros], axis=1)
    return cos, sin


def kernel(x, attn_norm, w_in, fox_f_bias, q_norm, w_uq, kv_norm, w_ukv, w_mla_branch,
           w_fox_branch, w_out, mlp_norm, w_up, w_down, final_norm):
    B, S, D = x.shape
    depth = w_in.shape[0]
    hd = N_HEADS * HEAD
    cos, sin = _rope_tables(S)
    mla_scale = LOG2E / math.sqrt(HEAD + ROPE)
    fox_scale = LOG2E / math.sqrt(HEAD)
    h = x
    for l in range(depth):
        w_a, w_g, w_fqk, w_fvt, w_q, w_k, w_vt, f_bias = _layer_weights(
            w_in[l], w_uq[l], w_ukv[l], fox_f_bias[l])
        (xn, q1, q2, k1, k2, vt, fvt, fq2, fk2) = _prep(
            h, attn_norm[l].reshape(1, D), w_a, q_norm[l].reshape(1, Q_LORA), w_q,
            kv_norm[l].reshape(1, KV_LORA), w_k, w_vt, w_fvt, f_bias, cos, sin, mla_scale)
        gates = _proj(xn, w_g, gate=True)
        fqk = _proj(xn, w_fqk, gate=False, n_scaled_cols=hd, scale=fox_scale)
        o_mla = _attention(q1, 0, q2, k1, 0, k2, True, vt, chunk_causal=True)
        o_fox = _attention(fqk, 0, fq2, fqk, N_HEADS, fk2, False, fvt, chunk_causal=False)
        h = _mix(o_mla, o_fox, gates, h, w_mla_branch[l].astype(BF16),
                 w_fox_branch[l].astype(BF16), w_out[l].astype(BF16))
        h = _mlp(h, mlp_norm[l].reshape(1, D), w_up[l].astype(BF16),
                 w_down[l].astype(BF16), final_norm.reshape(1, D),
                 final_norm=(l == depth - 1))
    return h
```

```python
import functools
import math

import jax
import jax.numpy as jnp
from jax import lax
from jax.experimental import pallas as pl
from jax.experimental.pallas import tpu as pltpu

F32 = jnp.float32
BF16 = jnp.bfloat16

EPS = 1e-6
LOG2E = math.log2(math.e)
ROPE_THETA = 10000.0
N_HEADS = 8
HEAD = 128
ROPE = 64
Q_LORA = 512
KV_LORA = 256
CHUNK = 64

LANES = 128
V7X_VMEM_LIMIT = 56 * 1024 * 1024

TILE_PREP = 512
TILE_ATTN = 512
V_ROWS = HEAD + 16
ATTN_HEADS = 4
TILE_PROJ_M = 1024
TILE_PROJ_N = 1024
TILE_GATE_N = 2048
TILE_MIX = 512
TILE_MLP_M = 1024
TILE_MLP_F = 512

_NT = (((1,), (1,)), ((), ()))


def _rms(v, g):
    return v * lax.rsqrt(jnp.mean(v * v, axis=-1, keepdims=True) + EPS) * g


def _params(sem, vmem=V7X_VMEM_LIMIT):
    return pltpu.CompilerParams(dimension_semantics=sem, vmem_limit_bytes=vmem)


def _resident(shape):
    nd = len(shape)
    return pl.BlockSpec(shape, lambda *_: (0,) * nd, pipeline_mode=pl.Buffered(1))


def _prep_kernel(x_ref, g_ref, wa_ref, qg_ref, wq_ref, kg_ref, wk_ref, wvt_ref,
                 wfvt_ref, fb_ref, cos_ref, sin_ref,
                 xn_ref, q1_ref, q2_ref, k1_ref, k2_ref, vt_ref, fvt_ref,
                 fq2_ref, fk2_ref, carry_ref, *, scale):
    tm = x_ref.shape[0]
    xn = _rms(x_ref[...], g_ref[...]).astype(BF16)
    xn_ref[...] = xn
    a = jnp.dot(xn, wa_ref[...], preferred_element_type=F32)
    cos = cos_ref[...]
    sin = sin_ref[...]

    ones_rows = (lax.broadcasted_iota(jnp.int32, (V_ROWS - HEAD, tm), 0) == 0).astype(BF16)

    def put_values(ref, vt):
        for h in range(N_HEADS):
            ref[h * V_ROWS:h * V_ROWS + HEAD, :] = vt[h * HEAD:(h + 1) * HEAD, :].astype(BF16)
            ref[h * V_ROWS + HEAD:(h + 1) * V_ROWS, :] = ones_rows

    def rope(r):
        return r * cos + pltpu.roll(r, ROPE, 1) * sin

    cqn = _rms(a[:, :Q_LORA], qg_ref[...]).astype(BF16)
    q = jnp.dot(cqn, wq_ref[...], preferred_element_type=F32)
    nq = N_HEADS * HEAD
    q1_ref[...] = (q[:, :nq] * scale).astype(BF16)
    for h in range(N_HEADS):
        r = q[:, nq + h * LANES: nq + (h + 1) * LANES]
        q2_ref[:, h * LANES:(h + 1) * LANES] = (rope(r) * scale).astype(BF16)

    ckvn = _rms(a[:, Q_LORA:Q_LORA + KV_LORA], kg_ref[...]).astype(BF16)
    k1_ref[...] = jnp.dot(ckvn, wk_ref[...], preferred_element_type=F32).astype(BF16)
    put_values(vt_ref, lax.dot_general(wvt_ref[...], ckvn, _NT, preferred_element_type=F32))
    off = Q_LORA + KV_LORA
    k2_ref[...] = rope(a[:, off:off + LANES]).astype(BF16)
    put_values(fvt_ref, lax.dot_general(wfvt_ref[...], xn, _NT, preferred_element_type=F32))

    z = a[:, off + LANES:off + 2 * LANES] + fb_ref[...]
    logf = jnp.minimum(z, 0.0) - jnp.log1p(jnp.exp(-jnp.abs(z)))

    @pl.when(pl.program_id(1) == 0)
    def _():
        carry_ref[...] = jnp.zeros_like(carry_ref)

    row = lax.broadcasted_iota(jnp.int32, (tm, tm), 0)
    col = lax.broadcasted_iota(jnp.int32, (tm, tm), 1)
    tri = (row >= col).astype(F32)
    cum = jnp.dot(tri, logf, preferred_element_type=F32,
                  precision=lax.Precision.HIGHEST) + carry_ref[...]
    carry_ref[...] = cum[tm - 1:tm, :]

    c = cum * LOG2E
    hi = c.astype(BF16)
    r1 = c - hi.astype(F32)
    mid = r1.astype(BF16)
    lo = (r1 - mid.astype(F32)).astype(BF16)
    src = lax.broadcasted_iota(jnp.int32, (LANES, LANES), 0)
    dst = lax.broadcasted_iota(jnp.int32, (LANES, LANES), 1)
    spread = ((dst >> 3) == src).astype(BF16)
    hi_s = jnp.dot(hi, spread, preferred_element_type=F32)
    mid_s = jnp.dot(mid, spread, preferred_element_type=F32)
    lo_s = jnp.dot(lo, spread, preferred_element_type=F32)
    lane = lax.broadcasted_iota(jnp.int32, (tm, LANES), 1)
    part = lane & 7
    owner = lane >> 3
    qa = jnp.where(part == 0, hi_s, jnp.where(part == 1, mid_s, jnp.where(
        part == 2, lo_s, jnp.where(part < 6, 1.0, 0.0))))
    ka = jnp.where(part < 3, 1.0, jnp.where(part == 3, -hi_s, jnp.where(
        part == 4, -mid_s, jnp.where(part == 5, -lo_s, 0.0))))
    fk2_ref[...] = jnp.where(owner < N_HEADS, ka, 0.0).astype(BF16)
    for h in range(N_HEADS):
        fq2_ref[:, h * LANES:(h + 1) * LANES] = jnp.where(owner == h, qa, 0.0).astype(BF16)


def _prep(x, attn_g, w_a, q_g, w_q, kv_g, w_k, w_vt, w_fvt, f_bias, cos, sin, scale):
    B, S, D = x.shape
    tm = min(TILE_PREP, S)
    nb = S // tm
    hd = N_HEADS * HEAD
    row = lambda w: pl.BlockSpec((None, tm, w), lambda b, i: (b, i, 0))
    vt_spec = pl.BlockSpec((None, None, N_HEADS * V_ROWS, tm), lambda b, i: (b, i, 0, 0))
    tab = pl.BlockSpec((tm, LANES), lambda b, i: (i, 0))
    out_shape = [
        jax.ShapeDtypeStruct((B, S, D), BF16),
        jax.ShapeDtypeStruct((B, S, hd), BF16),
        jax.ShapeDtypeStruct((B, S, hd), BF16),
        jax.ShapeDtypeStruct((B, S, hd), BF16),
        jax.ShapeDtypeStruct((B, S, LANES), BF16),
        jax.ShapeDtypeStruct((B, nb, N_HEADS * V_ROWS, tm), BF16),
        jax.ShapeDtypeStruct((B, nb, N_HEADS * V_ROWS, tm), BF16),
        jax.ShapeDtypeStruct((B, S, hd), BF16),
        jax.ShapeDtypeStruct((B, S, LANES), BF16),
    ]
    return pl.pallas_call(
        functools.partial(_prep_kernel, scale=scale),
        grid=(B, nb),
        in_specs=[row(D), _resident(attn_g.shape), _resident(w_a.shape),
                  _resident(q_g.shape), _resident(w_q.shape), _resident(kv_g.shape),
                  _resident(w_k.shape), _resident(w_vt.shape), _resident(w_fvt.shape),
                  _resident(f_bias.shape), tab, tab],
        out_specs=[row(D), row(hd), row(hd), row(hd), row(LANES), vt_spec, vt_spec,
                   row(hd), row(LANES)],
        out_shape=out_shape,
        scratch_shapes=[pltpu.VMEM((1, LANES), F32)],
        compiler_params=_params(("arbitrary", "arbitrary")),
    )(x, attn_g, w_a, q_g, w_q, kv_g, w_k, w_vt, w_fvt, f_bias, cos, sin)


def _proj_kernel(xn_ref, w_ref, o_ref, *, gate, n_scaled, scale):
    acc = jnp.dot(xn_ref[...], w_ref[...], preferred_element_type=F32)
    if gate:
        o_ref[...] = jax.nn.sigmoid(acc).astype(BF16)
    else:
        fac = jnp.where(pl.program_id(2) < n_scaled, scale, 1.0).astype(F32)
        o_ref[...] = (acc * fac).astype(BF16)


def _proj(xn, w, gate, n_scaled_cols=0, scale=1.0):
    B, S, D = xn.shape
    N = w.shape[1]
    tm = min(TILE_PROJ_M, S)
    tn = TILE_GATE_N if gate else TILE_PROJ_N
    return pl.pallas_call(
        functools.partial(_proj_kernel, gate=gate, n_scaled=n_scaled_cols // tn, scale=scale),
        grid=(B, S // tm, N // tn),
        in_specs=[pl.BlockSpec((None, tm, D), lambda b, i, j: (b, i, 0)),
                  pl.BlockSpec((D, tn), lambda b, i, j: (0, j))],
        out_specs=pl.BlockSpec((None, tm, tn), lambda b, i, j: (b, i, j)),
        out_shape=jax.ShapeDtypeStruct((B, S, N), BF16),
        compiler_params=_params(("arbitrary", "arbitrary", "arbitrary")),
    )(xn, w)


def _attn_kernel(q1_ref, q2_ref, k1_ref, k2_ref, vt_ref, o_ref, *scratch, chunk_causal):
    t = q1_ref.shape[0]
    i = pl.program_id(2)
    per_head = len(scratch) // ATTN_HEADS

    def lanes(ref, g, rows=slice(None)):
        return ref[rows, g * LANES:(g + 1) * LANES]

    class Head:
        def __init__(self, g):
            self.q, s0, s1, b0, b1, self.m, self.acc = scratch[g * per_head:(g + 1) * per_head]
            self.g = g
            self.s = (s0, s1)
            self.b = (b0, b1)

        def init(self):
            qq = jnp.concatenate([lanes(q1_ref, self.g), lanes(q2_ref, self.g)], axis=1)
            self.q[...] = qq.astype(F32).T.astype(BF16)
            self.m[...] = jnp.full_like(self.m, -jnp.inf)
            self.acc[...] = jnp.zeros_like(self.acc)

        def scores(self, kb, slot):
            rows = pl.ds(pl.multiple_of(kb * t, t), t)
            k = jnp.concatenate([lanes(k1_ref, self.g, rows), k2_ref[rows, :]], axis=1)
            s = jnp.dot(k, self.q[...], preferred_element_type=F32)
            self.s[slot][...] = s
            self.b[slot][...] = jnp.max(s, axis=0, keepdims=True)

        def softmax(self, slot, masked):
            s = self.s[slot][...]
            if masked:
                kj = lax.broadcasted_iota(jnp.int32, (t, t), 0)
                qi = lax.broadcasted_iota(jnp.int32, (t, t), 1)
                if chunk_causal:
                    shift = CHUNK.bit_length() - 1
                    valid = (kj >> shift) <= (qi >> shift)
                else:
                    valid = kj <= qi
                s = jnp.where(valid, s, -jnp.inf)
                blk_max = jnp.max(s, axis=0, keepdims=True)
            else:
                blk_max = self.b[slot][...]
            m_prev = self.m[...]
            m_new = jnp.maximum(m_prev, blk_max)
            alpha = jnp.exp2(m_prev - m_new)
            p = jnp.exp2(s - m_new)
            self.m[...] = m_new
            return p.astype(BF16), alpha

        def pv(self, kb, p, alpha):
            v = vt_ref[kb, self.g * V_ROWS:(self.g + 1) * V_ROWS, :]
            upd = jnp.dot(v, p, preferred_element_type=F32)
            self.acc[...] = alpha * self.acc[...] + upd

        def finish(self, cur):
            self.pv(i, *self.softmax(cur, True))
            out = self.acc[:HEAD, :] * (1.0 / self.acc[HEAD:HEAD + 1, :])
            o_ref[:, self.g * HEAD:(self.g + 1) * HEAD] = out.T.astype(BF16)

    heads = [Head(g) for g in range(ATTN_HEADS)]

    def iteration(kb, cur):
        for hd in heads:
            hd.scores(kb + 1, 1 - cur)
            hd.pv(kb, *hd.softmax(cur, False))

    for hd in heads:
        hd.init()
        hd.scores(0, 0)

    def pair(kk, carry):
        iteration(2 * kk, 0)
        iteration(2 * kk + 1, 1)
        return carry

    lax.fori_loop(0, i // 2, pair, 0)

    @pl.when(i % 2 == 1)
    def _():
        iteration(i - 1, 0)
        for hd in heads:
            hd.finish(1)

    @pl.when(i % 2 == 0)
    def _():
        for hd in heads:
            hd.finish(0)


def _attention(q1, q1_off, q2, k1, k1_off, k2, vt, chunk_causal):
    B, S = q1.shape[0], q1.shape[1]
    t = min(TILE_ATTN, S)
    nq = S // t
    w = ATTN_HEADS * LANES
    assert vt.shape[1] == nq and vt.shape[3] == t
    assert q1_off % ATTN_HEADS == 0 and k1_off % ATTN_HEADS == 0
    qspec = lambda off: pl.BlockSpec((None, t, w), lambda b, g, i: (b, i, g + off // ATTN_HEADS))
    kspec = lambda off: pl.BlockSpec((None, S, w), lambda b, g, i: (b, 0, g + off // ATTN_HEADS))
    k2spec = pl.BlockSpec((None, S, LANES), lambda b, g, i: (b, 0, 0))
    head_scratch = [pltpu.VMEM((2 * LANES, t), BF16),
                    pltpu.VMEM((t, t), F32), pltpu.VMEM((t, t), F32),
                    pltpu.VMEM((1, t), F32), pltpu.VMEM((1, t), F32),
                    pltpu.VMEM((1, t), F32),
                    pltpu.VMEM((V_ROWS, t), F32)]
    return pl.pallas_call(
        functools.partial(_attn_kernel, chunk_causal=chunk_causal),
        grid=(B, N_HEADS // ATTN_HEADS, nq),
        in_specs=[qspec(q1_off), qspec(0), kspec(k1_off), k2spec,
                  pl.BlockSpec((None, nq, ATTN_HEADS * V_ROWS, t), lambda b, g, i: (b, 0, g, 0))],
        out_specs=pl.BlockSpec((None, t, ATTN_HEADS * HEAD), lambda b, g, i: (b, i, g)),
        out_shape=jax.ShapeDtypeStruct((B, S, N_HEADS * HEAD), BF16),
        scratch_shapes=head_scratch * ATTN_HEADS,
        compiler_params=_params(("arbitrary", "arbitrary", "arbitrary")),
    )(q1, q2, k1, k2, vt)


def _mix_kernel(om_ref, of_ref, gm_ref, gf_ref, x_ref, wm_ref, wf_ref, wo_ref, h_ref):
    y_m = jnp.dot(om_ref[...], wm_ref[...], preferred_element_type=F32)
    y_f = jnp.dot(of_ref[...], wf_ref[...], preferred_element_type=F32)
    z = (gm_ref[...].astype(F32) * y_m + gf_ref[...].astype(F32) * y_f).astype(BF16)
    h_ref[...] = x_ref[...] + jnp.dot(z, wo_ref[...], preferred_element_type=F32)


def _mix(o_mla, o_fox, proj, x, w_m, w_f, w_o):
    B, S, D = x.shape
    tm = min(TILE_MIX, S)
    hd = o_mla.shape[2]
    row = lambda w, c=0: pl.BlockSpec((None, tm, w), lambda b, i: (b, i, c))
    return pl.pallas_call(
        _mix_kernel,
        grid=(B, S // tm),
        in_specs=[row(hd), row(hd), row(D, 0), row(D, 1), row(D),
                  _resident(w_m.shape), _resident(w_f.shape), _resident(w_o.shape)],
        out_specs=row(D),
        out_shape=jax.ShapeDtypeStruct((B, S, D), F32),
        compiler_params=_params(("arbitrary", "arbitrary")),
    )(o_mla, o_fox, proj, proj, x, w_m, w_f, w_o)


def _mlp_kernel(h_ref, g_ref, wu_ref, wd_ref, fg_ref, o_ref, hn_ref, *, final_norm):
    j = pl.program_id(2)

    def step(first):
        if first:
            hn_ref[...] = _rms(h_ref[...], g_ref[...]).astype(BF16)
        u = jnp.dot(hn_ref[...], wu_ref[...], preferred_element_type=F32)
        u = jnp.square(jnp.maximum(u, 0.0)).astype(BF16)
        d = jnp.dot(u, wd_ref[...], preferred_element_type=F32)
        o_ref[...] = (h_ref[...] if first else o_ref[...]) + d

    pl.when(j == 0)(lambda: step(True))
    pl.when(j > 0)(lambda: step(False))

    if final_norm:
        @pl.when(j == pl.num_programs(2) - 1)
        def _():
            o_ref[...] = _rms(o_ref[...], fg_ref[...])


def _mlp(h, mlp_g, w_up, w_down, final_g, final_norm):
    B, S, D = h.shape
    Fdim = w_up.shape[1]
    tm = min(TILE_MLP_M, S)
    tf = TILE_MLP_F
    row = pl.BlockSpec((None, tm, D), lambda b, i, j: (b, i, 0))
    return pl.pallas_call(
        functools.partial(_mlp_kernel, final_norm=final_norm),
        grid=(B, S // tm, Fdim // tf),
        in_specs=[row, _resident(mlp_g.shape),
                  pl.BlockSpec((D, tf), lambda b, i, j: (0, j)),
                  pl.BlockSpec((tf, D), lambda b, i, j: (j, 0)),
                  _resident(final_g.shape)],
        out_specs=row,
        out_shape=jax.ShapeDtypeStruct((B, S, D), F32),
        scratch_shapes=[pltpu.VMEM((tm, D), BF16)],
        compiler_params=_params(("arbitrary", "arbitrary", "arbitrary")),
    )(h, mlp_g, w_up, w_down, final_g)


def _layer_weights(w_in, w_uq, w_ukv, fox_f_bias):
    hd = N_HEADS * HEAD
    o_ckv = Q_LORA
    o_kr = o_ckv + KV_LORA
    o_fq = o_kr + ROPE
    o_fk = o_fq + hd
    o_fv = o_fk + hd
    o_ff = o_fv + hd
    o_g = o_ff + N_HEADS
    half = ROPE // 2
    w_in = w_in.astype(BF16)
    kr = w_in[:, o_kr:o_fq]
    kr4 = jnp.concatenate([kr, -kr[:, half:], kr[:, :half]], axis=1)
    ff = jnp.pad(w_in[:, o_ff:o_g], ((0, 0), (0, LANES - N_HEADS)))
    w_a = jnp.concatenate([w_in[:, :o_kr], kr4, ff], axis=1)
    w_g = w_in[:, o_g:]
    w_fqk = w_in[:, o_fq:o_fv]
    w_fvt = w_in[:, o_fv:o_ff].T

    uq = w_uq.reshape(Q_LORA, N_HEADS, HEAD + ROPE)
    x1 = uq[:, :, HEAD:HEAD + half]
    x2 = uq[:, :, HEAD + half:]
    rope4 = jnp.concatenate([x1, x2, -x2, x1], axis=2).reshape(Q_LORA, hd)
    w_q = jnp.concatenate([uq[:, :, :HEAD].reshape(Q_LORA, hd), rope4], axis=1).astype(BF16)

    ukv = w_ukv.reshape(KV_LORA, N_HEADS, 2 * HEAD)
    w_k = ukv[:, :, :HEAD].reshape(KV_LORA, hd).astype(BF16)
    w_vt = ukv[:, :, HEAD:].reshape(KV_LORA, hd).T.astype(BF16)
    f_bias = jnp.pad(fox_f_bias, (0, LANES - N_HEADS)).reshape(1, LANES)
    return w_a, w_g, w_fqk, w_fvt, w_q, w_k, w_vt, f_bias


def _rope_tables(S):
    pos = jnp.arange(S, dtype=F32)
    inv = 1.0 / (ROPE_THETA ** (jnp.arange(0, ROPE, 2, dtype=F32) / ROPE))
    ang = pos[:, None] * inv[None, :]
    zeros = jnp.zeros((S, LANES - ROPE), F32)
    cos = jnp.concatenate([jnp.cos(ang), jnp.cos(ang), zeros], axis=1)
    sin = jnp.concatenate([jnp.sin(ang), jnp.sin(ang), zeros], axis=1)
    return cos, sin


def kernel(x, attn_norm, w_in, fox_f_bias, q_norm, w_uq, kv_norm, w_ukv, w_mla_branch,
           w_fox_branch, w_out, mlp_norm, w_up, w_down, final_norm):
    B, S, D = x.shape
    depth = w_in.shape[0]
    hd = N_HEADS * HEAD
    cos, sin = _rope_tables(S)
    mla_scale = LOG2E / math.sqrt(HEAD + ROPE)
    fox_scale = LOG2E / math.sqrt(HEAD)
    h = x
    for l in range(depth):
        w_a, w_g, w_fqk, w_fvt, w_q, w_k, w_vt, f_bias = _layer_weights(
            w_in[l], w_uq[l], w_ukv[l], fox_f_bias[l])
        (xn, q1, q2, k1, k2, vt, fvt, fq2, fk2) = _prep(
            h, attn_norm[l].reshape(1, D), w_a, q_norm[l].reshape(1, Q_LORA), w_q,
            kv_norm[l].reshape(1, KV_LORA), w_k, w_vt, w_fvt, f_bias, cos, sin, mla_scale)
        gates = _proj(xn, w_g, gate=True)
        fqk = _proj(xn, w_fqk, gate=False, n_scaled_cols=hd, scale=fox_scale)
        o_mla = _attention(q1, 0, q2, k1, 0, k2, vt, chunk_causal=True)
        o_fox = _attention(fqk, 0, fq2, fqk, N_HEADS, fk2, fvt, chunk_causal=False)
        h = _mix(o_mla, o_fox, gates, h, w_mla_branch[l].astype(BF16),
                 w_fox_branch[l].astype(BF16), w_out[l].astype(BF16))
        h = _mlp(h, mlp_norm[l].reshape(1, D), w_up[l].astype(BF16),
                 w_down[l].astype(BF16), final_norm.reshape(1, D),
                 final_norm=(l == depth - 1))
    return h
```

```python
import functools
import math

import jax
import jax.numpy as jnp
from jax import lax
from jax.experimental import pallas as pl
from jax.experimental.pallas import tpu as pltpu

F32 = jnp.float32
BF16 = jnp.bfloat16

EPS = 1e-6
LOG2E = math.log2(math.e)
ROPE_THETA = 10000.0
N_HEADS = 8
HEAD = 128
ROPE = 64
Q_LORA = 512
KV_LORA = 256
CHUNK = 64

LANES = 128
V7X_VMEM_LIMIT = 56 * 1024 * 1024

TILE_PREP = 512
TILE_ATTN = 512
V_ROWS = HEAD + 16
ATTN_HEADS = 4
TILE_PROJ_M = 1024
TILE_PROJ_N = 1024
TILE_GATE_N = 2048
TILE_MIX = 512
TILE_MLP_M = 1024
TILE_MLP_F = 512

_NT = (((1,), (1,)), ((), ()))


def _rms(v, g):
    return v * lax.rsqrt(jnp.mean(v * v, axis=-1, keepdims=True) + EPS) * g


def _params(sem, vmem=V7X_VMEM_LIMIT):
    return pltpu.CompilerParams(dimension_semantics=sem, vmem_limit_bytes=vmem)


def _resident(shape):
    nd = len(shape)
    return pl.BlockSpec(shape, lambda *_: (0,) * nd, pipeline_mode=pl.Buffered(1))


def _prep_kernel(x_ref, g_ref, wa_ref, qg_ref, wq_ref, kg_ref, wk_ref, wvt_ref,
                 wfvt_ref, fb_ref, cos_ref, sin_ref,
                 xn_ref, q1_ref, q2_ref, k1_ref, k2_ref, vt_ref, fvt_ref,
                 fq2_ref, fk2_ref, carry_ref, *, scale):
    tm = x_ref.shape[0]
    xn = _rms(x_ref[...], g_ref[...]).astype(BF16)
    xn_ref[...] = xn
    a = jnp.dot(xn, wa_ref[...], preferred_element_type=F32)
    cos = cos_ref[...]
    sin = sin_ref[...]

    ones_rows = (lax.broadcasted_iota(jnp.int32, (V_ROWS - HEAD, tm), 0) == 0).astype(BF16)

    def put_values(ref, vt):
        for h in range(N_HEADS):
            ref[h * V_ROWS:h * V_ROWS + HEAD, :] = vt[h * HEAD:(h + 1) * HEAD, :].astype(BF16)
            ref[h * V_ROWS + HEAD:(h + 1) * V_ROWS, :] = ones_rows

    def rope(r):
        return r * cos + pltpu.roll(r, ROPE, 1) * sin

    cqn = _rms(a[:, :Q_LORA], qg_ref[...]).astype(BF16)
    q = jnp.dot(cqn, wq_ref[...], preferred_element_type=F32)
    nq = N_HEADS * HEAD
    q1_ref[...] = (q[:, :nq] * scale).astype(BF16)
    for h in range(N_HEADS):
        r = q[:, nq + h * LANES: nq + (h + 1) * LANES]
        q2_ref[:, h * LANES:(h + 1) * LANES] = (rope(r) * scale).astype(BF16)

    ckvn = _rms(a[:, Q_LORA:Q_LORA + KV_LORA], kg_ref[...]).astype(BF16)
    k1_ref[...] = jnp.dot(ckvn, wk_ref[...], preferred_element_type=F32).astype(BF16)
    put_values(vt_ref, lax.dot_general(wvt_ref[...], ckvn, _NT, preferred_element_type=F32))
    off = Q_LORA + KV_LORA
    k2_ref[...] = rope(a[:, off:off + LANES]).astype(BF16)
    put_values(fvt_ref, lax.dot_general(wfvt_ref[...], xn, _NT, preferred_element_type=F32))

    z = a[:, off + LANES:off + 2 * LANES] + fb_ref[...]
    logf = jnp.minimum(z, 0.0) - jnp.log1p(jnp.exp(-jnp.abs(z)))

    @pl.when(pl.program_id(1) == 0)
    def _():
        carry_ref[...] = jnp.zeros_like(carry_ref)

    row = lax.broadcasted_iota(jnp.int32, (tm, tm), 0)
    col = lax.broadcasted_iota(jnp.int32, (tm, tm), 1)
    tri = (row >= col).astype(BF16)
    l_hi = logf.astype(BF16)
    l_r = logf - l_hi.astype(F32)
    l_mid = l_r.astype(BF16)
    l_lo = (l_r - l_mid.astype(F32)).astype(BF16)
    cum = (jnp.dot(tri, l_hi, preferred_element_type=F32)
           + jnp.dot(tri, l_mid, preferred_element_type=F32)
           + jnp.dot(tri, l_lo, preferred_element_type=F32)) + carry_ref[...]
    carry_ref[...] = cum[tm - 1:tm, :]

    c = cum * LOG2E
    hi = c.astype(BF16)
    r1 = c - hi.astype(F32)
    mid = r1.astype(BF16)
    lo = (r1 - mid.astype(F32)).astype(BF16)
    src = lax.broadcasted_iota(jnp.int32, (LANES, LANES), 0)
    dst = lax.broadcasted_iota(jnp.int32, (LANES, LANES), 1)
    spread = ((dst >> 3) == src).astype(BF16)
    hi_s = jnp.dot(hi, spread, preferred_element_type=F32)
    mid_s = jnp.dot(mid, spread, preferred_element_type=F32)
    lo_s = jnp.dot(lo, spread, preferred_element_type=F32)
    lane = lax.broadcasted_iota(jnp.int32, (tm, LANES), 1)
    part = lane & 7
    owner = lane >> 3
    qa = jnp.where(part == 0, hi_s, jnp.where(part == 1, mid_s, jnp.where(
        part == 2, lo_s, jnp.where(part < 6, 1.0, 0.0))))
    ka = jnp.where(part < 3, 1.0, jnp.where(part == 3, -hi_s, jnp.where(
        part == 4, -mid_s, jnp.where(part == 5, -lo_s, 0.0))))
    fk2_ref[...] = jnp.where(owner < N_HEADS, ka, 0.0).astype(BF16)
    for h in range(N_HEADS):
        fq2_ref[:, h * LANES:(h + 1) * LANES] = jnp.where(owner == h, qa, 0.0).astype(BF16)


def _prep(x, attn_g, w_a, q_g, w_q, kv_g, w_k, w_vt, w_fvt, f_bias, cos, sin, scale):
    B, S, D = x.shape
    tm = min(TILE_PREP, S)
    nb = S // tm
    hd = N_HEADS * HEAD
    row = lambda w: pl.BlockSpec((None, tm, w), lambda b, i: (b, i, 0))
    vt_spec = pl.BlockSpec((None, None, N_HEADS * V_ROWS, tm), lambda b, i: (b, i, 0, 0))
    tab = pl.BlockSpec((tm, LANES), lambda b, i: (i, 0))
    out_shape = [
        jax.ShapeDtypeStruct((B, S, D), BF16),
        jax.ShapeDtypeStruct((B, S, hd), BF16),
        jax.ShapeDtypeStruct((B, S, hd), BF16),
        jax.ShapeDtypeStruct((B, S, hd), BF16),
        jax.ShapeDtypeStruct((B, S, LANES), BF16),
        jax.ShapeDtypeStruct((B, nb, N_HEADS * V_ROWS, tm), BF16),
        jax.ShapeDtypeStruct((B, nb, N_HEADS * V_ROWS, tm), BF16),
        jax.ShapeDtypeStruct((B, S, hd), BF16),
        jax.ShapeDtypeStruct((B, S, LANES), BF16),
    ]
    return pl.pallas_call(
        functools.partial(_prep_kernel, scale=scale),
        grid=(B, nb),
        in_specs=[row(D), _resident(attn_g.shape), _resident(w_a.shape),
                  _resident(q_g.shape), _resident(w_q.shape), _resident(kv_g.shape),
                  _resident(w_k.shape), _resident(w_vt.shape), _resident(w_fvt.shape),
                  _resident(f_bias.shape), tab, tab],
        out_specs=[row(D), row(hd), row(hd), row(hd), row(LANES), vt_spec, vt_spec,
                   row(hd), row(LANES)],
        out_shape=out_shape,
        scratch_shapes=[pltpu.VMEM((1, LANES), F32)],
        compiler_params=_params(("arbitrary", "arbitrary")),
    )(x, attn_g, w_a, q_g, w_q, kv_g, w_k, w_vt, w_fvt, f_bias, cos, sin)


def _proj_kernel(xn_ref, w_ref, o_ref, *, gate, n_scaled, scale):
    acc = jnp.dot(xn_ref[...], w_ref[...], preferred_element_type=F32)
    if gate:
        o_ref[...] = jax.nn.sigmoid(acc).astype(BF16)
    else:
        fac = jnp.where(pl.program_id(2) < n_scaled, scale, 1.0).astype(F32)
        o_ref[...] = (acc * fac).astype(BF16)


def _proj(xn, w, gate, n_scaled_cols=0, scale=1.0):
    B, S, D = xn.shape
    N = w.shape[1]
    tm = min(TILE_PROJ_M, S)
    tn = TILE_GATE_N if gate else TILE_PROJ_N
    return pl.pallas_call(
        functools.partial(_proj_kernel, gate=gate, n_scaled=n_scaled_cols // tn, scale=scale),
        grid=(B, S // tm, N // tn),
        in_specs=[pl.BlockSpec((None, tm, D), lambda b, i, j: (b, i, 0)),
                  pl.BlockSpec((D, tn), lambda b, i, j: (0, j))],
        out_specs=pl.BlockSpec((None, tm, tn), lambda b, i, j: (b, i, j)),
        out_shape=jax.ShapeDtypeStruct((B, S, N), BF16),
        compiler_params=_params(("arbitrary", "arbitrary", "arbitrary")),
    )(xn, w)


def _attn_kernel(q1_ref, q2_ref, k1_ref, k2_ref, vt_ref, o_ref, *scratch, chunk_causal):
    t = q1_ref.shape[0]
    i = pl.program_id(2)
    per_head = len(scratch) // ATTN_HEADS

    def lanes(ref, g, rows=slice(None)):
        return ref[rows, g * LANES:(g + 1) * LANES]

    class Head:
        def __init__(self, g):
            self.q, s0, s1, b0, b1, self.m, self.acc = scratch[g * per_head:(g + 1) * per_head]
            self.g = g
            self.s = (s0, s1)
            self.b = (b0, b1)

        def init(self):
            qq = jnp.concatenate([lanes(q1_ref, self.g), lanes(q2_ref, self.g)], axis=1)
            self.q[...] = qq.astype(F32).T.astype(BF16)
            self.m[...] = jnp.full_like(self.m, -jnp.inf)
            self.acc[...] = jnp.zeros_like(self.acc)

        def scores(self, kb, slot):
            rows = pl.ds(pl.multiple_of(kb * t, t), t)
            k = jnp.concatenate([lanes(k1_ref, self.g, rows), k2_ref[rows, :]], axis=1)
            s = jnp.dot(k, self.q[...], preferred_element_type=F32)
            self.s[slot][...] = s
            self.b[slot][...] = jnp.max(s, axis=0, keepdims=True)

        def softmax(self, slot, masked):
            s = self.s[slot][...]
            if masked:
                shift = CHUNK.bit_length() - 1 if chunk_causal else 0
                kj = lax.broadcasted_iota(jnp.int32, (t, 1), 0) >> shift
                qi = lax.broadcasted_iota(jnp.int32, (1, t), 1) >> shift
                valid = kj <= qi
                s = jnp.where(valid, s, -jnp.inf)
                blk_max = jnp.max(s, axis=0, keepdims=True)
            else:
                blk_max = self.b[slot][...]
            m_prev = self.m[...]
            m_new = jnp.maximum(m_prev, blk_max)
            alpha = jnp.exp2(m_prev - m_new)
            p = jnp.exp2(s - m_new)
            self.m[...] = m_new
            return p.astype(BF16), alpha

        def pv(self, kb, p, alpha):
            v = vt_ref[kb, self.g * V_ROWS:(self.g + 1) * V_ROWS, :]
            upd = jnp.dot(v, p, preferred_element_type=F32)
            self.acc[...] = alpha * self.acc[...] + upd

        def finish(self, cur):
            self.pv(i, *self.softmax(cur, True))
            out = self.acc[:HEAD, :] * (1.0 / self.acc[HEAD:HEAD + 1, :])
            o_ref[:, self.g * HEAD:(self.g + 1) * HEAD] = out.T.astype(BF16)

    heads = [Head(g) for g in range(ATTN_HEADS)]

    def iteration(kb, cur):
        for hd in heads:
            hd.scores(kb + 1, 1 - cur)
            hd.pv(kb, *hd.softmax(cur, False))

    for hd in heads:
        hd.init()
        hd.scores(0, 0)

    def pair(kk, carry):
        iteration(2 * kk, 0)
        iteration(2 * kk + 1, 1)
        return carry

    lax.fori_loop(0, i // 2, pair, 0)

    @pl.when(i % 2 == 1)
    def _():
        iteration(i - 1, 0)
        for hd in heads:
            hd.finish(1)

    @pl.when(i % 2 == 0)
    def _():
        for hd in heads:
            hd.finish(0)


def _attention(q1, q1_off, q2, k1, k1_off, k2, vt, chunk_causal):
    B, S = q1.shape[0], q1.shape[1]
    t = min(TILE_ATTN, S)
    nq = S // t
    w = ATTN_HEADS * LANES
    assert vt.shape[1] == nq and vt.shape[3] == t
    assert q1_off % ATTN_HEADS == 0 and k1_off % ATTN_HEADS == 0
    qspec = lambda off: pl.BlockSpec((None, t, w), lambda b, g, i: (b, i, g + off // ATTN_HEADS))
    kspec = lambda off: pl.BlockSpec((None, S, w), lambda b, g, i: (b, 0, g + off // ATTN_HEADS))
    k2spec = pl.BlockSpec((None, S, LANES), lambda b, g, i: (b, 0, 0))
    head_scratch = [pltpu.VMEM((2 * LANES, t), BF16),
                    pltpu.VMEM((t, t), F32), pltpu.VMEM((t, t), F32),
                    pltpu.VMEM((1, t), F32), pltpu.VMEM((1, t), F32),
                    pltpu.VMEM((1, t), F32),
                    pltpu.VMEM((V_ROWS, t), F32)]
    return pl.pallas_call(
        functools.partial(_attn_kernel, chunk_causal=chunk_causal),
        grid=(B, N_HEADS // ATTN_HEADS, nq),
        in_specs=[qspec(q1_off), qspec(0), kspec(k1_off), k2spec,
                  pl.BlockSpec((None, nq, ATTN_HEADS * V_ROWS, t), lambda b, g, i: (b, 0, g, 0))],
        out_specs=pl.BlockSpec((None, t, ATTN_HEADS * HEAD), lambda b, g, i: (b, i, g)),
        out_shape=jax.ShapeDtypeStruct((B, S, N_HEADS * HEAD), BF16),
        scratch_shapes=head_scratch * ATTN_HEADS,
        compiler_params=_params(("arbitrary", "arbitrary", "arbitrary")),
    )(q1, q2, k1, k2, vt)


def _mix_kernel(om_ref, of_ref, gm_ref, gf_ref, x_ref, wm_ref, wf_ref, wo_ref, h_ref):
    y_m = jnp.dot(om_ref[...], wm_ref[...], preferred_element_type=F32)
    y_f = jnp.dot(of_ref[...], wf_ref[...], preferred_element_type=F32)
    z = (gm_ref[...].astype(F32) * y_m + gf_ref[...].astype(F32) * y_f).astype(BF16)
    h_ref[...] = x_ref[...] + jnp.dot(z, wo_ref[...], preferred_element_type=F32)


def _mix(o_mla, o_fox, proj, x, w_m, w_f, w_o):
    B, S, D = x.shape
    tm = min(TILE_MIX, S)
    hd = o_mla.shape[2]
    row = lambda w, c=0: pl.BlockSpec((None, tm, w), lambda b, i: (b, i, c))
    return pl.pallas_call(
        _mix_kernel,
        grid=(B, S // tm),
        in_specs=[row(hd), row(hd), row(D, 0), row(D, 1), row(D),
                  _resident(w_m.shape), _resident(w_f.shape), _resident(w_o.shape)],
        out_specs=row(D),
        out_shape=jax.ShapeDtypeStruct((B, S, D), F32),
        compiler_params=_params(("arbitrary", "arbitrary")),
    )(o_mla, o_fox, proj, proj, x, w_m, w_f, w_o)


def _mlp_kernel(h_ref, g_ref, wu_ref, wd_ref, fg_ref, o_ref, hn_ref, *, final_norm):
    j = pl.program_id(2)

    def step(first):
        if first:
            hn_ref[...] = _rms(h_ref[...], g_ref[...]).astype(BF16)
        u = jnp.dot(hn_ref[...], wu_ref[...], preferred_element_type=F32)
        u = jnp.square(jnp.maximum(u, 0.0)).astype(BF16)
        d = jnp.dot(u, wd_ref[...], preferred_element_type=F32)
        o_ref[...] = (h_ref[...] if first else o_ref[...]) + d

    pl.when(j == 0)(lambda: step(True))
    pl.when(j > 0)(lambda: step(False))

    if final_norm:
        @pl.when(j == pl.num_programs(2) - 1)
        def _():
            o_ref[...] = _rms(o_ref[...], fg_ref[...])


def _mlp(h, mlp_g, w_up, w_down, final_g, final_norm):
    B, S, D = h.shape
    Fdim = w_up.shape[1]
    tm = min(TILE_MLP_M, S)
    tf = TILE_MLP_F
    row = pl.BlockSpec((None, tm, D), lambda b, i, j: (b, i, 0))
    return pl.pallas_call(
        functools.partial(_mlp_kernel, final_norm=final_norm),
        grid=(B, S // tm, Fdim // tf),
        in_specs=[row, _resident(mlp_g.shape),
                  pl.BlockSpec((D, tf), lambda b, i, j: (0, j)),
                  pl.BlockSpec((tf, D), lambda b, i, j: (j, 0)),
                  _resident(final_g.shape)],
        out_specs=row,
        out_shape=jax.ShapeDtypeStruct((B, S, D), F32),
        scratch_shapes=[pltpu.VMEM((tm, D), BF16)],
        compiler_params=_params(("arbitrary", "arbitrary", "arbitrary")),
    )(h, mlp_g, w_up, w_down, final_g)


def _layer_weights(w_in, w_uq, w_ukv, fox_f_bias):
    hd = N_HEADS * HEAD
    o_ckv = Q_LORA
    o_kr = o_ckv + KV_LORA
    o_fq = o_kr + ROPE
    o_fk = o_fq + hd
    o_fv = o_fk + hd
    o_ff = o_fv + hd
    o_g = o_ff + N_HEADS
    half = ROPE // 2
    w_in = w_in.astype(BF16)
    kr = w_in[:, o_kr:o_fq]
    kr4 = jnp.concatenate([kr, -kr[:, half:], kr[:, :half]], axis=1)
    ff = jnp.pad(w_in[:, o_ff:o_g], ((0, 0), (0, LANES - N_HEADS)))
    w_a = jnp.concatenate([w_in[:, :o_kr], kr4, ff], axis=1)
    w_g = w_in[:, o_g:]
    w_fqk = w_in[:, o_fq:o_fv]
    w_fvt = w_in[:, o_fv:o_ff].T

    uq = w_uq.reshape(Q_LORA, N_HEADS, HEAD + ROPE)
    x1 = uq[:, :, HEAD:HEAD + half]
    x2 = uq[:, :, HEAD + half:]
    rope4 = jnp.concatenate([x1, x2, -x2, x1], axis=2).reshape(Q_LORA, hd)
    w_q = jnp.concatenate([uq[:, :, :HEAD].reshape(Q_LORA, hd), rope4], axis=1).astype(BF16)

    ukv = w_ukv.reshape(KV_LORA, N_HEADS, 2 * HEAD)
    w_k = ukv[:, :, :HEAD].reshape(KV_LORA, hd).astype(BF16)
    w_vt = ukv[:, :, HEAD:].reshape(KV_LORA, hd).T.astype(BF16)
    f_bias = jnp.pad(fox_f_bias, (0, LANES - N_HEADS)).reshape(1, LANES)
    return w_a, w_g, w_fqk, w_fvt, w_q, w_k, w_vt, f_bias


def _rope_tables(S):
    pos = jnp.arange(S, dtype=F32)
    inv = 1.0 / (ROPE_THETA ** (jnp.arange(0, ROPE, 2, dtype=F32) / ROPE))
    ang = pos[:, None] * inv[None, :]
    zeros = jnp.zeros((S, LANES - ROPE), F32)
    cos = jnp.concatenate([jnp.cos(ang), jnp.cos(ang), zeros], axis=1)
    sin = jnp.concatenate([jnp.sin(ang), jnp.sin(ang), zeros], axis=1)
    return cos, sin


def kernel(x, attn_norm, w_in, fox_f_bias, q_norm, w_uq, kv_norm, w_ukv, w_mla_branch,
           w_fox_branch, w_out, mlp_norm, w_up, w_down, final_norm):
    B, S, D = x.shape
    depth = w_in.shape[0]
    hd = N_HEADS * HEAD
    cos, sin = _rope_tables(S)
    mla_scale = LOG2E / math.sqrt(HEAD + ROPE)
    fox_scale = LOG2E / math.sqrt(HEAD)
    h = x
    for l in range(depth):
        w_a, w_g, w_fqk, w_fvt, w_q, w_k, w_vt, f_bias = _layer_weights(
            w_in[l], w_uq[l], w_ukv[l], fox_f_bias[l])
        (xn, q1, q2, k1, k2, vt, fvt, fq2, fk2) = _prep(
            h, attn_norm[l].reshape(1, D), w_a, q_norm[l].reshape(1, Q_LORA), w_q,
            kv_norm[l].reshape(1, KV_LORA), w_k, w_vt, w_fvt, f_bias, cos, sin, mla_scale)
        gates = _proj(xn, w_g, gate=True)
        fqk = _proj(xn, w_fqk, gate=False, n_scaled_cols=hd, scale=fox_scale)
        o_mla = _attention(q1, 0, q2, k1, 0, k2, vt, chunk_causal=True)
        o_fox = _attention(fqk, 0, fq2, fqk, N_HEADS, fk2, fvt, chunk_causal=False)
        h = _mix(o_mla, o_fox, gates, h, w_mla_branch[l].astype(BF16),
                 w_fox_branch[l].astype(BF16), w_out[l].astype(BF16))
        h = _mlp(h, mlp_norm[l].reshape(1, D), w_up[l].astype(BF16),
                 w_down[l].astype(BF16), final_norm.reshape(1, D),
                 final_norm=(l == depth - 1))
    return h
```

```python
import functools
import math

import jax
import jax.numpy as jnp
from jax import lax
from jax.experimental import pallas as pl
from jax.experimental.pallas import tpu as pltpu

F32 = jnp.float32
BF16 = jnp.bfloat16

EPS = 1e-6
LOG2E = math.log2(math.e)
ROPE_THETA = 10000.0
N_HEADS = 8
HEAD = 128
ROPE = 64
Q_LORA = 512
KV_LORA = 256
CHUNK = 64

LANES = 128
V7X_VMEM_LIMIT = 56 * 1024 * 1024
V7X_MLP_VMEM_LIMIT = 59 * 1024 * 1024

TILE_PREP = 512
TILE_ATTN = 512
V_ROWS = HEAD + 16
ATTN_HEADS = 4
TILE_PROJ_M = 1024
TILE_PROJ_N = 2048
TILE_MIX = 512
TILE_MLP_M = 1024
TILE_MLP_F = 1024

_NT = (((1,), (1,)), ((), ()))


def _rms(v, g):
    return v * lax.rsqrt(jnp.mean(v * v, axis=-1, keepdims=True) + EPS) * g


def _params(sem, vmem=V7X_VMEM_LIMIT):
    return pltpu.CompilerParams(dimension_semantics=sem, vmem_limit_bytes=vmem)


def _resident(shape):
    nd = len(shape)
    return pl.BlockSpec(shape, lambda *_: (0,) * nd, pipeline_mode=pl.Buffered(1))


def _prep_kernel(x_ref, g_ref, wa_ref, qg_ref, wq_ref, kg_ref, wk_ref, wvt_ref,
                 wfvt_ref, fb_ref, cos_ref, sin_ref,
                 xn_ref, q1_ref, q2_ref, k1_ref, k2_ref, vt_ref, fvt_ref,
                 fq2_ref, fk2_ref, carry_ref, *, scale):
    tm = x_ref.shape[0]
    xn = _rms(x_ref[...], g_ref[...]).astype(BF16)
    xn_ref[...] = xn
    a = jnp.dot(xn, wa_ref[...], preferred_element_type=F32)
    cos = cos_ref[...]
    sin = sin_ref[...]

    ones_rows = (lax.broadcasted_iota(jnp.int32, (V_ROWS - HEAD, tm), 0) == 0).astype(BF16)

    def put_values(ref, vt):
        for h in range(N_HEADS):
            ref[h * V_ROWS:h * V_ROWS + HEAD, :] = vt[h * HEAD:(h + 1) * HEAD, :].astype(BF16)
            ref[h * V_ROWS + HEAD:(h + 1) * V_ROWS, :] = ones_rows

    def rope(r):
        return r * cos + pltpu.roll(r, ROPE, 1) * sin

    cqn = _rms(a[:, :Q_LORA], qg_ref[...]).astype(BF16)
    q = jnp.dot(cqn, wq_ref[...], preferred_element_type=F32)
    nq = N_HEADS * HEAD
    q1_ref[...] = (q[:, :nq] * scale).astype(BF16)
    for h in range(N_HEADS):
        r = q[:, nq + h * LANES: nq + (h + 1) * LANES]
        q2_ref[:, h * LANES:(h + 1) * LANES] = (rope(r) * scale).astype(BF16)

    ckvn = _rms(a[:, Q_LORA:Q_LORA + KV_LORA], kg_ref[...]).astype(BF16)
    k1_ref[...] = jnp.dot(ckvn, wk_ref[...], preferred_element_type=F32).astype(BF16)
    put_values(vt_ref, lax.dot_general(wvt_ref[...], ckvn, _NT, preferred_element_type=F32))
    off = Q_LORA + KV_LORA
    k2_ref[...] = rope(a[:, off:off + LANES]).astype(BF16)
    put_values(fvt_ref, lax.dot_general(wfvt_ref[...], xn, _NT, preferred_element_type=F32))

    z = a[:, off + LANES:off + 2 * LANES] + fb_ref[...]
    logf = jnp.minimum(z, 0.0) - jnp.log1p(jnp.exp(-jnp.abs(z)))

    @pl.when(pl.program_id(1) == 0)
    def _():
        carry_ref[...] = jnp.zeros_like(carry_ref)

    row = lax.broadcasted_iota(jnp.int32, (tm, tm), 0)
    col = lax.broadcasted_iota(jnp.int32, (tm, tm), 1)
    tri = (row >= col).astype(BF16)
    l_hi = logf.astype(BF16)
    l_r = logf - l_hi.astype(F32)
    l_mid = l_r.astype(BF16)
    l_lo = (l_r - l_mid.astype(F32)).astype(BF16)
    cum = (jnp.dot(tri, l_hi, preferred_element_type=F32)
           + jnp.dot(tri, l_mid, preferred_element_type=F32)
           + jnp.dot(tri, l_lo, preferred_element_type=F32)) + carry_ref[...]
    carry_ref[...] = cum[tm - 1:tm, :]

    c = cum * LOG2E
    hi = c.astype(BF16)
    r1 = c - hi.astype(F32)
    mid = r1.astype(BF16)
    lo = (r1 - mid.astype(F32)).astype(BF16)
    src = lax.broadcasted_iota(jnp.int32, (LANES, LANES), 0)
    dst = lax.broadcasted_iota(jnp.int32, (LANES, LANES), 1)
    spread = ((dst >> 3) == src).astype(BF16)
    hi_s = jnp.dot(hi, spread, preferred_element_type=F32)
    mid_s = jnp.dot(mid, spread, preferred_element_type=F32)
    lo_s = jnp.dot(lo, spread, preferred_element_type=F32)
    lane = lax.broadcasted_iota(jnp.int32, (tm, LANES), 1)
    part = lane & 7
    owner = lane >> 3
    qa = jnp.where(part == 0, hi_s, jnp.where(part == 1, mid_s, jnp.where(
        part == 2, lo_s, jnp.where(part < 6, 1.0, 0.0))))
    ka = jnp.where(part < 3, 1.0, jnp.where(part == 3, -hi_s, jnp.where(
        part == 4, -mid_s, jnp.where(part == 5, -lo_s, 0.0))))
    fk2_ref[...] = jnp.where(owner < N_HEADS, ka, 0.0).astype(BF16)
    for h in range(N_HEADS):
        fq2_ref[:, h * LANES:(h + 1) * LANES] = jnp.where(owner == h, qa, 0.0).astype(BF16)


def _prep(x, attn_g, w_a, q_g, w_q, kv_g, w_k, w_vt, w_fvt, f_bias, cos, sin, scale):
    B, S, D = x.shape
    tm = min(TILE_PREP, S)
    nb = S // tm
    hd = N_HEADS * HEAD
    row = lambda w: pl.BlockSpec((None, tm, w), lambda b, i: (b, i, 0))
    vt_spec = pl.BlockSpec((None, None, N_HEADS * V_ROWS, tm), lambda b, i: (b, i, 0, 0))
    tab = pl.BlockSpec((tm, LANES), lambda b, i: (i, 0))
    out_shape = [
        jax.ShapeDtypeStruct((B, S, D), BF16),
        jax.ShapeDtypeStruct((B, S, hd), BF16),
        jax.ShapeDtypeStruct((B, S, hd), BF16),
        jax.ShapeDtypeStruct((B, S, hd), BF16),
        jax.ShapeDtypeStruct((B, S, LANES), BF16),
        jax.ShapeDtypeStruct((B, nb, N_HEADS * V_ROWS, tm), BF16),
        jax.ShapeDtypeStruct((B, nb, N_HEADS * V_ROWS, tm), BF16),
        jax.ShapeDtypeStruct((B, S, hd), BF16),
        jax.ShapeDtypeStruct((B, S, LANES), BF16),
    ]
    return pl.pallas_call(
        functools.partial(_prep_kernel, scale=scale),
        grid=(B, nb),
        in_specs=[row(D), _resident(attn_g.shape), _resident(w_a.shape),
                  _resident(q_g.shape), _resident(w_q.shape), _resident(kv_g.shape),
                  _resident(w_k.shape), _resident(w_vt.shape), _resident(w_fvt.shape),
                  _resident(f_bias.shape), tab, tab],
        out_specs=[row(D), row(hd), row(hd), row(hd), row(LANES), vt_spec, vt_spec,
                   row(hd), row(LANES)],
        out_shape=out_shape,
        scratch_shapes=[pltpu.VMEM((1, LANES), F32)],
        compiler_params=_params(("arbitrary", "arbitrary")),
    )(x, attn_g, w_a, q_g, w_q, kv_g, w_k, w_vt, w_fvt, f_bias, cos, sin)


def _proj_kernel(xn_ref, w_ref, o_ref, *, gate, n_scaled, scale):
    acc = jnp.dot(xn_ref[...], w_ref[...], preferred_element_type=F32)
    if gate:
        o_ref[...] = (0.5 * jnp.tanh(0.5 * acc) + 0.5).astype(BF16)
    else:
        tn = o_ref.shape[1]
        col = pl.program_id(2) * tn + lax.broadcasted_iota(jnp.int32, (1, tn), 1)
        o_ref[...] = (acc * jnp.where(col < n_scaled, scale, 1.0)).astype(BF16)


def _proj(xn, w, gate, n_scaled_cols=0, scale=1.0):
    B, S, D = xn.shape
    N = w.shape[1]
    tm = min(TILE_PROJ_M, S)
    tn = TILE_PROJ_N
    return pl.pallas_call(
        functools.partial(_proj_kernel, gate=gate, n_scaled=n_scaled_cols, scale=scale),
        grid=(B, S // tm, N // tn),
        in_specs=[pl.BlockSpec((None, tm, D), lambda b, i, j: (b, i, 0)),
                  pl.BlockSpec((D, tn), lambda b, i, j: (0, j))],
        out_specs=pl.BlockSpec((None, tm, tn), lambda b, i, j: (b, i, j)),
        out_shape=jax.ShapeDtypeStruct((B, S, N), BF16),
        compiler_params=_params(("arbitrary", "arbitrary", "arbitrary")),
    )(xn, w)


def _attn_kernel(q1_ref, q2_ref, k1_ref, k2_ref, vt_ref, o_ref, *scratch, chunk_causal):
    t = q1_ref.shape[0]
    i = pl.program_id(2)
    per_head = len(scratch) // ATTN_HEADS

    def lanes(ref, g, rows=slice(None)):
        return ref[rows, g * LANES:(g + 1) * LANES]

    class Head:
        def __init__(self, g):
            self.q, s0, s1, b0, b1, self.m, self.acc = scratch[g * per_head:(g + 1) * per_head]
            self.g = g
            self.s = (s0, s1)
            self.b = (b0, b1)

        def init(self):
            qq = jnp.concatenate([lanes(q1_ref, self.g), lanes(q2_ref, self.g)], axis=1)
            self.q[...] = qq.astype(F32).T.astype(BF16)
            self.m[...] = jnp.full_like(self.m, -jnp.inf)
            self.acc[...] = jnp.zeros_like(self.acc)

        def scores(self, kb, slot):
            rows = pl.ds(pl.multiple_of(kb * t, t), t)
            k = jnp.concatenate([lanes(k1_ref, self.g, rows), k2_ref[rows, :]], axis=1)
            s = jnp.dot(k, self.q[...], preferred_element_type=F32)
            self.s[slot][...] = s
            self.b[slot][...] = jnp.max(s, axis=0, keepdims=True)

        def softmax(self, slot, masked):
            s = self.s[slot][...]
            if masked:
                shift = CHUNK.bit_length() - 1 if chunk_causal else 0
                kj = lax.broadcasted_iota(jnp.int32, (t, 1), 0) >> shift
                qi = lax.broadcasted_iota(jnp.int32, (1, t), 1) >> shift
                valid = kj <= qi
                s = jnp.where(valid, s, -jnp.inf)
                blk_max = jnp.max(s, axis=0, keepdims=True)
            else:
                blk_max = self.b[slot][...]
            m_prev = self.m[...]
            m_new = jnp.maximum(m_prev, blk_max)
            alpha = jnp.exp2(m_prev - m_new)
            p = jnp.exp2(s - m_new)
            self.m[...] = m_new
            return p.astype(BF16), alpha

        def pv(self, kb, p, alpha):
            v = vt_ref[kb, self.g * V_ROWS:(self.g + 1) * V_ROWS, :]
            upd = jnp.dot(v, p, preferred_element_type=F32)
            self.acc[...] = alpha * self.acc[...] + upd

        def finish(self, cur):
            self.pv(i, *self.softmax(cur, True))
            out = self.acc[:HEAD, :] * (1.0 / self.acc[HEAD:HEAD + 1, :])
            o_ref[:, self.g * HEAD:(self.g + 1) * HEAD] = out.T.astype(BF16)

    heads = [Head(g) for g in range(ATTN_HEADS)]

    def iteration(kb, cur):
        for hd in heads:
            hd.scores(kb + 1, 1 - cur)
            hd.pv(kb, *hd.softmax(cur, False))

    for hd in heads:
        hd.init()
        hd.scores(0, 0)

    def pair(kk, carry):
        iteration(2 * kk, 0)
        iteration(2 * kk + 1, 1)
        return carry

    lax.fori_loop(0, i // 2, pair, 0)

    @pl.when(i % 2 == 1)
    def _():
        iteration(i - 1, 0)
        for hd in heads:
            hd.finish(1)

    @pl.when(i % 2 == 0)
    def _():
        for hd in heads:
            hd.finish(0)


def _attention(q1, q1_off, q2, k1, k1_off, k2, vt, chunk_causal):
    B, S = q1.shape[0], q1.shape[1]
    t = min(TILE_ATTN, S)
    nq = S // t
    w = ATTN_HEADS * LANES
    assert vt.shape[1] == nq and vt.shape[3] == t
    assert q1_off % ATTN_HEADS == 0 and k1_off % ATTN_HEADS == 0
    qspec = lambda off: pl.BlockSpec((None, t, w), lambda b, g, i: (b, i, g + off // ATTN_HEADS))
    kspec = lambda off: pl.BlockSpec((None, S, w), lambda b, g, i: (b, 0, g + off // ATTN_HEADS))
    k2spec = pl.BlockSpec((None, S, LANES), lambda b, g, i: (b, 0, 0))
    head_scratch = [pltpu.VMEM((2 * LANES, t), BF16),
                    pltpu.VMEM((t, t), F32), pltpu.VMEM((t, t), F32),
                    pltpu.VMEM((1, t), F32), pltpu.VMEM((1, t), F32),
                    pltpu.VMEM((1, t), F32),
                    pltpu.VMEM((V_ROWS, t), F32)]
    return pl.pallas_call(
        functools.partial(_attn_kernel, chunk_causal=chunk_causal),
        grid=(B, N_HEADS // ATTN_HEADS, nq),
        in_specs=[qspec(q1_off), qspec(0), kspec(k1_off), k2spec,
                  pl.BlockSpec((None, nq, ATTN_HEADS * V_ROWS, t), lambda b, g, i: (b, 0, g, 0))],
        out_specs=pl.BlockSpec((None, t, ATTN_HEADS * HEAD), lambda b, g, i: (b, i, g)),
        out_shape=jax.ShapeDtypeStruct((B, S, N_HEADS * HEAD), BF16),
        scratch_shapes=head_scratch * ATTN_HEADS,
        compiler_params=_params(("arbitrary", "arbitrary", "arbitrary")),
    )(q1, q2, k1, k2, vt)


def _mix_kernel(om_ref, of_ref, gm_ref, gf_ref, x_ref, wm_ref, wf_ref, wo_ref, h_ref):
    y_m = jnp.dot(om_ref[...], wm_ref[...], preferred_element_type=F32)
    y_f = jnp.dot(of_ref[...], wf_ref[...], preferred_element_type=F32)
    z = (gm_ref[...].astype(F32) * y_m + gf_ref[...].astype(F32) * y_f).astype(BF16)
    h_ref[...] = x_ref[...] + jnp.dot(z, wo_ref[...], preferred_element_type=F32)


def _mix(o_mla, o_fox, proj, x, w_m, w_f, w_o):
    B, S, D = x.shape
    tm = min(TILE_MIX, S)
    hd = o_mla.shape[2]
    row = lambda w, c=0: pl.BlockSpec((None, tm, w), lambda b, i: (b, i, c))
    return pl.pallas_call(
        _mix_kernel,
        grid=(B, S // tm),
        in_specs=[row(hd), row(hd), row(D, 0), row(D, 1), row(D),
                  _resident(w_m.shape), _resident(w_f.shape), _resident(w_o.shape)],
        out_specs=row(D),
        out_shape=jax.ShapeDtypeStruct((B, S, D), F32),
        compiler_params=_params(("arbitrary", "arbitrary")),
    )(o_mla, o_fox, proj, proj, x, w_m, w_f, w_o)


def _mlp_kernel(h_ref, g_ref, wu_ref, wd_ref, fg_ref, o_ref, hn_ref, *, final_norm):
    j = pl.program_id(2)

    def step(first):
        if first:
            hn_ref[...] = _rms(h_ref[...], g_ref[...]).astype(BF16)
        u = jnp.dot(hn_ref[...], wu_ref[...], preferred_element_type=F32)
        u = jnp.square(jnp.maximum(u, 0.0)).astype(BF16)
        d = jnp.dot(u, wd_ref[...], preferred_element_type=F32)
        o_ref[...] = (h_ref[...] if first else o_ref[...]) + d

    pl.when(j == 0)(lambda: step(True))
    pl.when(j > 0)(lambda: step(False))

    if final_norm:
        @pl.when(j == pl.num_programs(2) - 1)
        def _():
            o_ref[...] = _rms(o_ref[...], fg_ref[...])


def _mlp(h, mlp_g, w_up, w_down, final_g, final_norm):
    B, S, D = h.shape
    Fdim = w_up.shape[1]
    tm = min(TILE_MLP_M, S)
    tf = TILE_MLP_F
    row = pl.BlockSpec((None, tm, D), lambda b, i, j: (b, i, 0))
    return pl.pallas_call(
        functools.partial(_mlp_kernel, final_norm=final_norm),
        grid=(B, S // tm, Fdim // tf),
        in_specs=[row, _resident(mlp_g.shape),
                  pl.BlockSpec((D, tf), lambda b, i, j: (0, j)),
                  pl.BlockSpec((tf, D), lambda b, i, j: (j, 0)),
                  _resident(final_g.shape)],
        out_specs=row,
        out_shape=jax.ShapeDtypeStruct((B, S, D), F32),
        scratch_shapes=[pltpu.VMEM((tm, D), BF16)],
        compiler_params=_params(("arbitrary", "arbitrary", "arbitrary"), vmem=V7X_MLP_VMEM_LIMIT),
    )(h, mlp_g, w_up, w_down, final_g)


def _layer_weights(w_in, w_uq, w_ukv, fox_f_bias):
    hd = N_HEADS * HEAD
    o_ckv = Q_LORA
    o_kr = o_ckv + KV_LORA
    o_fq = o_kr + ROPE
    o_fk = o_fq + hd
    o_fv = o_fk + hd
    o_ff = o_fv + hd
    o_g = o_ff + N_HEADS
    half = ROPE // 2
    w_in = w_in.astype(BF16)
    kr = w_in[:, o_kr:o_fq]
    kr4 = jnp.concatenate([kr, -kr[:, half:], kr[:, :half]], axis=1)
    ff = jnp.pad(w_in[:, o_ff:o_g], ((0, 0), (0, LANES - N_HEADS)))
    w_a = jnp.concatenate([w_in[:, :o_kr], kr4, ff], axis=1)
    w_g = w_in[:, o_g:]
    w_fqk = w_in[:, o_fq:o_fv]
    w_fvt = w_in[:, o_fv:o_ff].T

    uq = w_uq.reshape(Q_LORA, N_HEADS, HEAD + ROPE)
    x1 = uq[:, :, HEAD:HEAD + half]
    x2 = uq[:, :, HEAD + half:]
    rope4 = jnp.concatenate([x1, x2, -x2, x1], axis=2).reshape(Q_LORA, hd)
    w_q = jnp.concatenate([uq[:, :, :HEAD].reshape(Q_LORA, hd), rope4], axis=1).astype(BF16)

    ukv = w_ukv.reshape(KV_LORA, N_HEADS, 2 * HEAD)
    w_k = ukv[:, :, :HEAD].reshape(KV_LORA, hd).astype(BF16)
    w_vt = ukv[:, :, HEAD:].reshape(KV_LORA, hd).T.astype(BF16)
    f_bias = jnp.pad(fox_f_bias, (0, LANES - N_HEADS)).reshape(1, LANES)
    return w_a, w_g, w_fqk, w_fvt, w_q, w_k, w_vt, f_bias


def _rope_tables(S):
    pos = jnp.arange(S, dtype=F32)
    inv = 1.0 / (ROPE_THETA ** (jnp.arange(0, ROPE, 2, dtype=F32) / ROPE))
    ang = pos[:, None] * inv[None, :]
    zeros = jnp.zeros((S, LANES - ROPE), F32)
    cos = jnp.concatenate([jnp.cos(ang), jnp.cos(ang), zeros], axis=1)
    sin = jnp.concatenate([jnp.sin(ang), jnp.sin(ang), zeros], axis=1)
    return cos, sin


def kernel(x, attn_norm, w_in, fox_f_bias, q_norm, w_uq, kv_norm, w_ukv, w_mla_branch,
           w_fox_branch, w_out, mlp_norm, w_up, w_down, final_norm):
    B, S, D = x.shape
    depth = w_in.shape[0]
    hd = N_HEADS * HEAD
    cos, sin = _rope_tables(S)
    mla_scale = LOG2E / math.sqrt(HEAD + ROPE)
    fox_scale = LOG2E / math.sqrt(HEAD)
    h = x
    for l in range(depth):
        w_a, w_g, w_fqk, w_fvt, w_q, w_k, w_vt, f_bias = _layer_weights(
            w_in[l], w_uq[l], w_ukv[l], fox_f_bias[l])
        (xn, q1, q2, k1, k2, vt, fvt, fq2, fk2) = _prep(
            h, attn_norm[l].reshape(1, D), w_a, q_norm[l].reshape(1, Q_LORA), w_q,
            kv_norm[l].reshape(1, KV_LORA), w_k, w_vt, w_fvt, f_bias, cos, sin, mla_scale)
        gates = _proj(xn, w_g, gate=True)
        fqk = _proj(xn, w_fqk, gate=False, n_scaled_cols=hd, scale=fox_scale)
        o_mla = _attention(q1, 0, q2, k1, 0, k2, vt, chunk_causal=True)
        o_fox = _attention(fqk, 0, fq2, fqk, N_HEADS, fk2, fvt, chunk_causal=False)
        h = _mix(o_mla, o_fox, gates, h, w_mla_branch[l].astype(BF16),
                 w_fox_branch[l].astype(BF16), w_out[l].astype(BF16))
        h = _mlp(h, mlp_norm[l].reshape(1, D), w_up[l].astype(BF16),
                 w_down[l].astype(BF16), final_norm.reshape(1, D),
                 final_norm=(l == depth - 1))
    return h
```
